```python
import jax, jax.numpy as jnp
from jax import lax
import numpy as np

D_MODEL = 1024
BATCH = 8
SEQ = 2048
DEPTH = 1

CTX_LEN = 256
GRID_W = 64
D_FF = ((8 * D_MODEL // 3 + 255) // 256) * 256
D_FOURIER = D_MODEL // 2
FOURIER_GROUPS = 4
FOURIER_GROUP_W = D_FOURIER // FOURIER_GROUPS
D_LRU = D_MODEL
LRU_HEADS = 8
LRU_HEAD_W = D_LRU // LRU_HEADS
CONV_W = 4
CONV_LEFT = (CONV_W - 1) // 2
GATE_C = 8.0
N_MOD = 9
EPS = 1e-6
D_IN = D_FOURIER + 2 * D_LRU + 2 * D_MODEL
SPLITS = (D_FOURIER, D_FOURIER + D_LRU, D_FOURIER + 2 * D_LRU, D_FOURIER + 2 * D_LRU + D_MODEL)

kernel_name = "hybrid_fnet_rglru_prefix_dit_layer"


def rms_norm(x, g):
    xf = x.astype(jnp.float32)
    y = xf * lax.rsqrt(jnp.mean(xf * xf, axis=-1, keepdims=True) + EPS)
    return (y * g.astype(jnp.float32)).astype(x.dtype)


def modulate(x, g, shift, scale):
    return rms_norm(x, g) * (1.0 + scale) + shift


def swiglu(h, w_in, w_out):
    gate, up = jnp.split(h @ w_in, 2, axis=-1)
    return (jax.nn.silu(gate) * up) @ w_out


def fourier_latent(f):
    b, L, _ = f.shape
    rows = L // GRID_W
    z = f.astype(jnp.float32).reshape(b, rows, GRID_W, FOURIER_GROUPS, FOURIER_GROUP_W)
    z = jnp.fft.fftn(z, axes=(1, 2, 4), norm="ortho").real
    return z.reshape(b, L, D_FOURIER).astype(f.dtype)


def fourier_context(f):
    b, L, _ = f.shape
    z = f.astype(jnp.float32).reshape(b, L, FOURIER_GROUPS, FOURIER_GROUP_W)
    z = jnp.fft.fftn(z, axes=(1, 3), norm="ortho").real
    return z.reshape(b, L, D_FOURIER).astype(f.dtype)


def dwconv_centred(x, w, b):
    L = x.shape[1]
    xp = jnp.pad(x, ((0, 0), (CONV_LEFT, CONV_W - 1 - CONV_LEFT), (0, 0)))
    y = b
    for k in range(CONV_W):
        y = y + xp[:, k:k + L] * w[k]
    return y


def block_diag(x, w, b):
    bsz, L, C = x.shape
    xh = x.reshape(bsz, L, LRU_HEADS, LRU_HEAD_W)
    return jnp.einsum("blhi,hij->blhj", xh, w).reshape(bsz, L, C) + b


def linear_scan(a, b, h0):
    b = b.at[:, 0].add(a[:, 0] * h0)

    def combine(left, right):
        return (left[0] * right[0], right[0] * left[1] + right[1])

    _, h = lax.associative_scan(combine, (a, b), axis=1)
    return h


def rglru_direction(xc, w_r, b_r, w_i, b_i, lam, h0, reverse):
    if reverse:
        xc = jnp.flip(xc, axis=1)
    r = jax.nn.sigmoid(block_diag(xc, w_r, b_r)).astype(jnp.float32)
    i = jax.nn.sigmoid(block_diag(xc, w_i, b_i)).astype(jnp.float32)
    log_a = -GATE_C * r * jax.nn.softplus(-lam.astype(jnp.float32))
    a = jnp.exp(log_a)
    bterm = jnp.sqrt(-jnp.expm1(2.0 * log_a)) * (i * xc.astype(jnp.float32))
    h = linear_scan(a, bterm, h0)
    h_last = h[:, -1]
    if reverse:
        h = jnp.flip(h, axis=1)
    return h, h_last


def mixer(hx, hc, w_in, conv_w, conv_b, w_r, b_r, w_i, b_i, lam, w_fa, w_fb, w_out, need_ctx_out):
    fx, lx, gx, gax, gbx = jnp.split(hx @ w_in, SPLITS, axis=-1)
    fc, lc, gc, gac, gbc = jnp.split(hc @ w_in, SPLITS, axis=-1)

    xcx = dwconv_centred(lx, conv_w, conv_b)
    xcc = dwconv_centred(lc, conv_w, conv_b)
    h0 = jnp.zeros((xcc.shape[0], D_LRU), jnp.float32)
    hcf, hcf_last = rglru_direction(xcc, w_r[0], b_r[0], w_i[0], b_i[0], lam[0], h0, False)
    hcb, hcb_last = rglru_direction(xcc, w_r[1], b_r[1], w_i[1], b_i[1], lam[1], h0, True)
    hxf, _ = rglru_direction(xcx, w_r[0], b_r[0], w_i[0], b_i[0], lam[0], hcf_last, False)
    hxb, _ = rglru_direction(xcx, w_r[1], b_r[1], w_i[1], b_i[1], lam[1], hcb_last, True)

    yb_x = ((hxf + hxb).astype(hx.dtype) * jax.nn.gelu(gx)) @ w_fb
    ya_x = fourier_latent(fx) @ w_fa
    out_x = (jax.nn.sigmoid(gax) * ya_x + jax.nn.sigmoid(gbx) * yb_x) @ w_out
    if not need_ctx_out:
        return out_x, None
    yb_c = ((hcf + hcb).astype(hc.dtype) * jax.nn.gelu(gc)) @ w_fb
    ya_c = fourier_context(fc) @ w_fa
    out_c = (jax.nn.sigmoid(gac) * ya_c + jax.nn.sigmoid(gbc) * yb_c) @ w_out
    return out_x, out_c


def setup_inputs(seed: int = 0) -> dict:
    key = jax.random.key(seed)
    ks = jax.random.split(key, 24)
    D, L = D_MODEL, DEPTH

    def nrm(k, shape, fan_in):
        return jax.random.normal(k, shape, jnp.float32) * (fan_in ** -0.5)

    a0 = jax.random.uniform(ks[19], (L, 2, D_LRU), jnp.float32, 0.9, 0.999)
    s = a0 ** (1.0 / GATE_C)
    lam = jnp.log(s) - jnp.log1p(-s)
    return {
        "x": jax.random.normal(ks[0], (BATCH, SEQ, D), jnp.float32),
        "c": jax.random.normal(ks[1], (BATCH, D), jnp.float32),
        "ctx": jax.random.normal(ks[2], (BATCH, CTX_LEN, D), jnp.float32),
        "c_ctx": jax.random.normal(ks[3], (D,), jnp.float32),
        "w_ada": nrm(ks[4], (L, D, N_MOD * D), D),
        "b_ada": 0.01 * jax.random.normal(ks[5], (L, N_MOD * D), jnp.float32),
        "norm_g": 1.0 + 0.05 * jax.random.normal(ks[6], (L, 6, D), jnp.float32),
        "w_ffn1_in": nrm(ks[7], (L, D, 2 * D_FF), D),
        "w_ffn1_out": nrm(ks[8], (L, D_FF, D), D_FF),
        "w_ffn2_in": nrm(ks[9], (L, D, 2 * D_FF), D),
        "w_ffn2_out": nrm(ks[10], (L, D_FF, D), D_FF),
        "w_in": nrm(ks[11], (L, D, D_IN), D),
        "conv_w": nrm(ks[12], (L, CONV_W, D_LRU), CONV_W),
        "conv_b": 0.01 * jax.random.normal(ks[13], (L, D_LRU), jnp.float32),
        "w_r": nrm(ks[14], (L, 2, LRU_HEADS, LRU_HEAD_W, LRU_HEAD_W), LRU_HEAD_W),
        "b_r": 0.01 * jax.random.normal(ks[15], (L, 2, D_LRU), jnp.float32),
        "w_i": nrm(ks[16], (L, 2, LRU_HEADS, LRU_HEAD_W, LRU_HEAD_W), LRU_HEAD_W),
        "b_i": 0.01 * jax.random.normal(ks[17], (L, 2, D_LRU), jnp.float32),
        "lam": lam,
        "w_fa": nrm(ks[20], (L, D_FOURIER, D), D_FOURIER),
        "w_fb": nrm(ks[21], (L, D_LRU, D), D_LRU),
        "w_out": nrm(ks[22], (L, D, D), D),
    }


def reference(x, c, ctx, c_ctx, w_ada, b_ada, norm_g, w_ffn1_in, w_ffn1_out, w_ffn2_in, w_ffn2_out,
              w_in, conv_w, conv_b, w_r, b_r, w_i, b_i, lam, w_fa, w_fb, w_out):
    sc = jax.nn.silu(c)
    sc_ctx = jax.nn.silu(c_ctx)
    for l in range(DEPTH):
        last = l == DEPTH - 1
        g = norm_g[l]
        mod_x = jnp.split((sc @ w_ada[l] + b_ada[l])[:, None, :], N_MOD, axis=-1)
        mod_c = jnp.split((sc_ctx @ w_ada[l] + b_ada[l])[None, None, :], N_MOD, axis=-1)
        sh1x, sc1x, ga1x, sh2x, sc2x, ga2x, sh3x, sc3x, ga3x = mod_x
        sh1c, sc1c, ga1c, sh2c, sc2c, ga2c, sh3c, sc3c, ga3c = mod_c

        x = x + 0.5 * ga1x * rms_norm(swiglu(modulate(x, g[0], sh1x, sc1x), w_ffn1_in[l], w_ffn1_out[l]), g[1])
        ctx = ctx + 0.5 * ga1c * rms_norm(swiglu(modulate(ctx, g[0], sh1c, sc1c), w_ffn1_in[l], w_ffn1_out[l]), g[1])

        mx, mc = mixer(modulate(x, g[2], sh2x, sc2x), modulate(ctx, g[2], sh2c, sc2c),
                       w_in[l], conv_w[l], conv_b[l], w_r[l], b_r[l], w_i[l], b_i[l], lam[l],
                       w_fa[l], w_fb[l], w_out[l], not last)
        x = x + ga2x * rms_norm(mx, g[3])

        x = x + 0.5 * ga3x * rms_norm(swiglu(modulate(x, g[4], sh3x, sc3x), w_ffn2_in[l], w_ffn2_out[l]), g[5])
        if not last:
            ctx = ctx + ga2c * rms_norm(mc, g[3])
            ctx = ctx + 0.5 * ga3c * rms_norm(swiglu(modulate(ctx, g[4], sh3c, sc3c), w_ffn2_in[l], w_ffn2_out[l]), g[5])
    return x
```

```python
import functools
import math

import numpy as np
import jax
import jax.numpy as jnp
from jax import lax
from jax.experimental import pallas as pl
from jax.experimental.pallas import tpu as pltpu

LANES = 128
SUBLANES = 8
MXU_DIM = 256
VMEM_LIMIT_BYTES = 56 * 1024 * 1024

GRID_W = 64
FOURIER_GROUPS = 4
LRU_HEADS = 8
CONV_W = 4
CONV_LEFT = (CONV_W - 1) // 2
GATE_C = 8.0
N_MOD = 9
EPS = 1e-6

BF16 = jnp.bfloat16
F32 = jnp.float32

SEG_LEN = 4
SCAN_ROWS = SEG_LEN * SUBLANES


def _sigmoid(x):
    return 0.5 * jnp.tanh(0.5 * x) + 0.5


def _silu(x):
    return x * _sigmoid(x)


def _gelu_tanh(x):
    c = math.sqrt(2.0 / math.pi)
    return 0.5 * x * (1.0 + jnp.tanh(c * (x + 0.044715 * (x * x * x))))


def _rms(x, g):
    return x * lax.rsqrt(jnp.mean(x * x, axis=-1, keepdims=True) + EPS) * g


def _resident(shape):
    nd = len(shape)
    return pl.BlockSpec(shape, lambda *_: (0,) * nd, pipeline_mode=pl.Buffered(1))


def _params(semantics):
    return pltpu.CompilerParams(dimension_semantics=semantics, vmem_limit_bytes=VMEM_LIMIT_BYTES)


def _ada_kernel(c_ref, w_ref, b_ref, o_ref):
    s = _silu(c_ref[...]).astype(BF16)
    o_ref[...] = jnp.dot(s, w_ref[...].astype(BF16), preferred_element_type=F32) + b_ref[...]


def _ada(cc, w, b, tn=1536):
    m, d = cc.shape
    n = w.shape[1]
    return pl.pallas_call(
        _ada_kernel,
        grid=(n // tn,),
        in_specs=[pl.BlockSpec((m, d), lambda j: (0, 0)),
                  pl.BlockSpec((d, tn), lambda j: (0, j)),
                  pl.BlockSpec((1, tn), lambda j: (0, j))],
        out_specs=pl.BlockSpec((m, tn), lambda j: (0, j)),
        out_shape=jax.ShapeDtypeStruct((m, n), F32),
        compiler_params=_params(("arbitrary",)),
        name="adaln",
    )(cc, w, b)


def _ffn_kernel(x_ref, sh_ref, sc_ref, ga_ref, gpre_ref, gpost_ref, wi_ref, wo_ref,
                o_ref, act_ref, *, chunk):
    dff = wo_ref.shape[0]
    x = x_ref[0]
    h = (_rms(x, gpre_ref[...]) * (1.0 + sc_ref[0]) + sh_ref[0]).astype(BF16)
    for c in range(0, dff, chunk):
        gate = jnp.dot(h, wi_ref[:, c:c + chunk], preferred_element_type=F32)
        up = jnp.dot(h, wi_ref[:, dff + c:dff + c + chunk], preferred_element_type=F32)
        act_ref[:, c:c + chunk] = (_silu(gate) * up).astype(BF16)
    y = jnp.dot(act_ref[...], wo_ref[...], preferred_element_type=F32)
    o_ref[0] = x + (0.5 * ga_ref[0]) * _rms(y, gpost_ref[...])


def _ffn(x, mod, mod_row, k_shift, g_pre, g_post, wi, wo, tm=512, chunk=MXU_DIM):
    bsz, seq, d = x.shape
    dff = wo.shape[0]
    mod_spec = lambda k: pl.BlockSpec((1, 1, d), lambda b, i: (mod_row(b), 0, k))
    return pl.pallas_call(
        functools.partial(_ffn_kernel, chunk=chunk),
        grid=(bsz, seq // tm),
        in_specs=[pl.BlockSpec((1, tm, d), lambda b, i: (b, i, 0)),
                  mod_spec(k_shift), mod_spec(k_shift + 1), mod_spec(k_shift + 2),
                  _resident((1, d)), _resident((1, d)),
                  _resident(wi.shape), _resident(wo.shape)],
        out_specs=pl.BlockSpec((1, tm, d), lambda b, i: (b, i, 0)),
        out_shape=jax.ShapeDtypeStruct(x.shape, F32),
        scratch_shapes=[pltpu.VMEM((tm, dff), BF16)],
        compiler_params=_params(("arbitrary", "arbitrary")),
        name="ffn",
    )(x, mod, mod, mod, g_pre, g_post, wi, wo)


def _inproj_kernel(x_ref, sh_ref, sc_ref, g_ref, w_ref, *o_refs, splits, chunk):
    h = (_rms(x_ref[0], g_ref[...]) * (1.0 + sc_ref[0]) + sh_ref[0]).astype(BF16)
    for o_ref, (start, width) in zip(o_refs, splits):
        for c in range(0, width, chunk):
            r = jnp.dot(h, w_ref[:, start + c:start + c + chunk], preferred_element_type=F32)
            o_ref[0, :, c:c + chunk] = r.astype(o_ref.dtype)


def _inproj(x, mod, mod_row, k_shift, g, w, splits, dtypes, tm=512, chunk=512):
    bsz, seq, d = x.shape
    mod_spec = lambda k: pl.BlockSpec((1, 1, d), lambda b, i: (mod_row(b), 0, k))
    return pl.pallas_call(
        functools.partial(_inproj_kernel, splits=splits, chunk=chunk),
        grid=(bsz, seq // tm),
        in_specs=[pl.BlockSpec((1, tm, d), lambda b, i: (b, i, 0)),
                  mod_spec(k_shift), mod_spec(k_shift + 1),
                  _resident((1, d)), _resident(w.shape)],
        out_specs=[pl.BlockSpec((1, tm, wd), lambda b, i: (b, i, 0)) for _, wd in splits],
        out_shape=[jax.ShapeDtypeStruct((bsz, seq, wd), dt) for (_, wd), dt in zip(splits, dtypes)],
        compiler_params=_params(("arbitrary", "arbitrary")),
        name="inproj",
    )(x, mod, mod, g, w)


def _shift_rows(x, k, fill, rows):
    return jnp.where(rows >= k, pltpu.roll(x, k, axis=0), fill)


def _shift_rows_up(x, k, fill, rows):
    return jnp.where(rows < SUBLANES - k, pltpu.roll(x, SUBLANES - k, axis=0), fill)


def _scan_group(a_ref, b_ref, o_ref, j, base, carry, rows, reverse):
    ks = range(SEG_LEN - 1, -1, -1) if reverse else range(SEG_LEN)
    shift = _shift_rows_up if reverse else _shift_rows
    a = {k: a_ref[j, pl.ds(base + k * SUBLANES, SUBLANES), :] for k in ks}
    b = {k: b_ref[j, pl.ds(base + k * SUBLANES, SUBLANES), :] for k in ks}
    h, p = {}, {}
    prev = None
    for k in ks:
        if prev is None:
            h[k], p[k] = b[k], a[k]
        else:
            h[k] = a[k] * h[prev] + b[k]
            p[k] = a[k] * p[prev]
        prev = k
    hc, pc = h[prev], p[prev]
    for dd in (1, 2, 4):
        hc = pc * shift(hc, dd, 0.0, rows) + hc
        pc = pc * shift(pc, dd, 1.0, rows)
    g = hc + pc * carry
    cin = shift(g, 1, carry, rows)
    for k in ks:
        o_ref[j, pl.ds(base + k, SUBLANES, stride=SEG_LEN), :] = h[k] + p[k] * cin
    last = 0 if reverse else SUBLANES - 1
    return jnp.broadcast_to(g[last:last + 1, :], (SUBLANES, LANES))


def _rec_kernel(l_ref, lp_ref, ln_ref, cw_ref, cb_ref, wg_ref, br_ref, bi_ref, lam_ref, h0_ref,
                o_ref, ext_ref, a_ref, b_ref, ho_ref, carry_ref, *, tt, nc):
    d = pl.program_id(0)
    c = pl.program_id(2)
    tc = c + d * (nc - 1 - 2 * c)
    n_slab = a_ref.shape[0]

    lam = lam_ref[0]
    e = jnp.exp(-jnp.abs(lam))
    u = 1.0 + e
    log1p_e = jnp.where(u == 1.0, e, jnp.log(u) * (e / (u - 1.0)))
    c1 = (-0.5 * GATE_C * math.log2(math.e)) * (jnp.maximum(-lam, 0.0) + log1p_e)

    for j in range(n_slab):
        sl = slice(j * LANES, (j + 1) * LANES)
        ext_ref[j, 0:SUBLANES, :] = jnp.where(tc == 0, 0.0, lp_ref[0, :, sl])
        ext_ref[j, SUBLANES:SUBLANES + tt, :] = l_ref[0, :, sl]
        ext_ref[j, SUBLANES + tt:2 * SUBLANES + tt, :] = jnp.where(tc == nc - 1, 0.0, ln_ref[0, :, sl])
        xc = cb_ref[:, sl]
        for k in range(CONV_W):
            off = SUBLANES - CONV_LEFT + k
            tap = jnp.concatenate(
                [ext_ref[j, pl.ds(off + m + kk, SUBLANES, stride=SEG_LEN), :]
                 for m in range(0, tt, SCAN_ROWS) for kk in range(SEG_LEN)], axis=0)
            xc = xc + tap * cw_ref[k:k + 1, sl]
        z = jnp.dot(xc.astype(BF16), wg_ref[0, j], preferred_element_type=F32)
        t_r = jnp.tanh(z[:, :LANES] + br_ref[0, :, sl])
        t_i = jnp.tanh(z[:, LANES:] + bi_ref[0, :, sl])
        a = jnp.exp2(c1[:, sl] * t_r + c1[:, sl])
        y = 1.0 - a * a
        root = jnp.where(y > 0.0, y * lax.rsqrt(y), 0.0)
        hx = 0.5 * xc
        a_ref[j] = a
        b_ref[j] = root * (t_i * hx + hx)

    @pl.when(c == 0)
    def _():
        for j in range(n_slab):
            carry_ref[j] = jnp.broadcast_to(h0_ref[0, 0, :, j * LANES:(j + 1) * LANES], (SUBLANES, LANES))

    rows = lax.broadcasted_iota(jnp.int32, (SUBLANES, LANES), 0)
    n_groups = tt // SCAN_ROWS

    def run(reverse):
        def body(m, carries):
            base = pl.multiple_of(((n_groups - 1 - m) if reverse else m) * SCAN_ROWS, SCAN_ROWS)
            return tuple(_scan_group(a_ref, b_ref, ho_ref, j, base, carries[j], rows, reverse)
                         for j in range(n_slab))
        carries = lax.fori_loop(0, n_groups, body, tuple(carry_ref[j] for j in range(n_slab)))
        for j in range(n_slab):
            carry_ref[j] = carries[j]

    @pl.when(d == 0)
    def _():
        run(False)

    @pl.when(d == 1)
    def _():
        run(True)

    for j in range(n_slab):
        o_ref[0, 0, :, j * LANES:(j + 1) * LANES] = ho_ref[j].astype(o_ref.dtype)


def _rglru(l, conv_w, conv_b, wg, br, bi, lam, h0, out_dtype, tt=256):
    bsz, t, ch = l.shape
    nc = t // tt
    n_slab = ch // LANES
    hb = tt // SUBLANES
    tcf = lambda d, c: c + d * (nc - 1 - 2 * c)
    vec = lambda: pl.BlockSpec((1, 1, ch), lambda d, b, c: (d, 0, 0))
    return pl.pallas_call(
        functools.partial(_rec_kernel, tt=tt, nc=nc),
        grid=(2, bsz, nc),
        in_specs=[pl.BlockSpec((1, tt, ch), lambda d, b, c: (b, tcf(d, c), 0)),
                  pl.BlockSpec((1, SUBLANES, ch), lambda d, b, c: (b, jnp.maximum(tcf(d, c) * hb - 1, 0), 0)),
                  pl.BlockSpec((1, SUBLANES, ch),
                               lambda d, b, c: (b, jnp.minimum((tcf(d, c) + 1) * hb, t // SUBLANES - 1), 0)),
                  pl.BlockSpec(conv_w.shape, lambda d, b, c: (0, 0)),
                  pl.BlockSpec((1, ch), lambda d, b, c: (0, 0)),
                  pl.BlockSpec((1,) + wg.shape[1:], lambda d, b, c: (d, 0, 0, 0)),
                  vec(), vec(), vec(),
                  pl.BlockSpec((1, 1, 1, ch), lambda d, b, c: (d, b, 0, 0))],
        out_specs=pl.BlockSpec((1, 1, tt, ch), lambda d, b, c: (d, b, tcf(d, c), 0)),
        out_shape=jax.ShapeDtypeStruct((2, bsz, t, ch), out_dtype),
        scratch_shapes=[pltpu.VMEM((n_slab, tt + 2 * SUBLANES, LANES), F32),
                        pltpu.VMEM((n_slab, tt, LANES), F32),
                        pltpu.VMEM((n_slab, tt, LANES), F32),
                        pltpu.VMEM((n_slab, tt, LANES), F32),
                        pltpu.VMEM((n_slab, SUBLANES, LANES), F32)],
        compiler_params=_params(("arbitrary", "arbitrary", "arbitrary")),
        name="rglru",
    )(l, l, l, conv_w, conv_b, wg, br, bi, lam, h0)


def _dft_tables(rows, cols, group_w, groups):
    n_tok = rows * cols
    scale = 1.0 / math.sqrt(n_tok * group_w)
    ch = np.arange(group_w)
    ph = 2.0 * np.pi * ((ch[:, None] * ch[None, :]) % group_w) / group_w
    d_model = groups * group_w
    dc = np.zeros((d_model, 2 * d_model), np.float64)
    for gi in range(groups):
        s = slice(gi * group_w, (gi + 1) * group_w)
        dc[s, gi * group_w:(gi + 1) * group_w] = np.cos(ph)
        dc[s, d_model + gi * group_w:d_model + (gi + 1) * group_w] = np.sin(ph)
    t = np.arange(n_tok)
    r, c = t // cols, t % cols
    th = 2.0 * np.pi * (((r[:, None] * r[None, :]) % rows) / rows + ((c[:, None] * c[None, :]) % cols) / cols)
    wc = np.cos(th) * scale
    ws = -np.sin(th) * scale
    as_bf16 = lambda m: jnp.asarray(m.astype(np.float32)).astype(BF16)
    return as_bf16(dc), as_bf16(wc), as_bf16(ws)


def _fourier_kernel(f_ref, dc_ref, wc_ref, ws_ref, o_ref, z_ref, *, tm):
    n_tok, df = f_ref.shape[1], f_ref.shape[2]
    for m in range(0, n_tok, tm):
        z = jnp.dot(f_ref[0, m:m + tm, :], dc_ref[...], preferred_element_type=F32)
        z_ref[m:m + tm, :] = z.astype(BF16)
    for m in range(0, n_tok, tm):
        o = jnp.dot(wc_ref[m:m + tm, :], z_ref[:, :df], preferred_element_type=F32)
        o = o + jnp.dot(ws_ref[m:m + tm, :], z_ref[:, df:], preferred_element_type=F32)
        o_ref[0, m:m + tm, :] = o.astype(o_ref.dtype)


def _fourier(f, dc, wc, ws, tm=512):
    bsz, n_tok, df = f.shape
    return pl.pallas_call(
        functools.partial(_fourier_kernel, tm=tm),
        grid=(bsz,),
        in_specs=[pl.BlockSpec((1, n_tok, df), lambda b: (b, 0, 0)),
                  _resident(dc.shape), _resident(wc.shape), _resident(ws.shape)],
        out_specs=pl.BlockSpec((1, n_tok, df), lambda b: (b, 0, 0)),
        out_shape=jax.ShapeDtypeStruct(f.shape, BF16),
        scratch_shapes=[pltpu.VMEM((n_tok, 2 * df), BF16)],
        compiler_params=_params(("arbitrary",)),
        name="fourier",
    )(f, dc, wc, ws)


def _merge_kernel(x_ref, h_ref, g_ref, ga_ref, gb_ref, fo_ref, gate_ref, gn_ref, wfa_ref, wfb_ref, wo_ref, o_ref):
    hsum = h_ref[0, 0].astype(F32) + h_ref[1, 0].astype(F32)
    u = (hsum * _gelu_tanh(g_ref[0].astype(F32))).astype(BF16)
    yb = jnp.dot(u, wfb_ref[...], preferred_element_type=F32)
    ya = jnp.dot(fo_ref[0], wfa_ref[...], preferred_element_type=F32)
    m = _sigmoid(ga_ref[0].astype(F32)) * ya + _sigmoid(gb_ref[0].astype(F32)) * yb
    mx = jnp.dot(m.astype(BF16), wo_ref[...], preferred_element_type=F32)
    o_ref[0] = x_ref[0] + gate_ref[0] * _rms(mx, gn_ref[...])


def _merge(x, h, g, ga, gb, fo, mod, k_gate, gn, wfa, wfb, wo, tm=512):
    bsz, seq, d = x.shape
    tok = lambda w: pl.BlockSpec((1, tm, w), lambda b, i: (b, i, 0))
    return pl.pallas_call(
        _merge_kernel,
        grid=(bsz, seq // tm),
        in_specs=[tok(d),
                  pl.BlockSpec((2, 1, tm, d), lambda b, i: (0, b, i, 0)),
                  tok(d), tok(d), tok(d), tok(fo.shape[2]),
                  pl.BlockSpec((1, 1, d), lambda b, i: (b, 0, k_gate)),
                  _resident((1, d)), _resident(wfa.shape), _resident(wfb.shape), _resident(wo.shape)],
        out_specs=tok(d),
        out_shape=jax.ShapeDtypeStruct(x.shape, F32),
        compiler_params=_params(("arbitrary", "arbitrary")),
        name="merge",
    )(x, h, g, ga, gb, fo, mod, gn, wfa, wfb, wo)


def kernel(x, c, ctx, c_ctx, w_ada, b_ada, norm_g, w_ffn1_in, w_ffn1_out, w_ffn2_in, w_ffn2_out,
           w_in, conv_w, conv_b, w_r, b_r, w_i, b_i, lam, w_fa, w_fb, w_out):
    depth = w_ada.shape[0]
    assert depth == 1, "single-layer problem: the context stream is only needed up to the mixer scans"
    bsz, seq, d = x.shape
    d_lru = w_fb.shape[1]
    d_f = w_fa.shape[1]
    lyr = 0

    pad = (-(bsz + 1)) % SUBLANES
    cc = jnp.concatenate([c, c_ctx[None, :], jnp.zeros((pad, d), F32)], axis=0)
    mod = _ada(cc, w_ada[lyr], b_ada[lyr][None, :])
    mod = mod.reshape(mod.shape[0], 1, N_MOD * d)
    lat_row = lambda b: b
    ctx_row = lambda b: bsz
    g = norm_g[lyr][:, None, :]

    wi1, wo1 = w_ffn1_in[lyr].astype(BF16), w_ffn1_out[lyr].astype(BF16)
    wi2, wo2 = w_ffn2_in[lyr].astype(BF16), w_ffn2_out[lyr].astype(BF16)
    w_in_b = w_in[lyr].astype(BF16)

    x1 = _ffn(x, mod, lat_row, 0, g[0], g[1], wi1, wo1)
    c1 = _ffn(ctx, mod, ctx_row, 0, g[0], g[1], wi1, wo1, tm=ctx.shape[1])

    splits = ((0, d_f), (d_f, d_lru), (d_f + d_lru, d_lru), (d_f + 2 * d_lru, d), (d_f + 2 * d_lru + d, d))
    fx, lx, gx, gax, gbx = _inproj(x1, mod, lat_row, 3, g[2], w_in_b, splits, (BF16, F32, BF16, BF16, BF16))
    (lc,) = _inproj(c1, mod, ctx_row, 3, g[2], w_in_b[:, d_f:d_f + d_lru], ((0, d_lru),), (F32,),
                    tm=ctx.shape[1])

    wgate = (0.5 * jnp.concatenate([w_r[lyr], w_i[lyr]], axis=-1)).astype(BF16)
    br, bi, lm = 0.5 * b_r[lyr][:, None, :], 0.5 * b_i[lyr][:, None, :], lam[lyr][:, None, :]
    cw, cb = conv_w[lyr], conv_b[lyr][None, :]
    h_ctx = _rglru(lc, cw, cb, wgate, br, bi, lm, jnp.zeros((2, bsz, 1, d_lru), F32), F32,
                   tt=ctx.shape[1])
    h0 = jnp.stack([h_ctx[0, :, -1], h_ctx[1, :, 0]])[:, :, None, :]
    h_lat = _rglru(lx, cw, cb, wgate, br, bi, lm, h0, F32)

    dc, wc, ws = _dft_tables(seq // GRID_W, GRID_W, d_f // FOURIER_GROUPS, FOURIER_GROUPS)
    fo = _fourier(fx, dc, wc, ws)

    x2 = _merge(x1, h_lat, gx, gax, gbx, fo, mod, 5, g[3],
                w_fa[lyr].astype(BF16), w_fb[lyr].astype(BF16), w_out[lyr].astype(BF16))
    return _ffn(x2, mod, lat_row, 6, g[4], g[5], wi2, wo2)
```

```python
import functools
import math

import numpy as np
import jax
import jax.numpy as jnp
from jax import lax
from jax.experimental import pallas as pl
from jax.experimental.pallas import tpu as pltpu

LANES = 128
SUBLANES = 8
MXU_DIM = 256
VMEM_LIMIT_BYTES = 56 * 1024 * 1024
TAIL_VMEM_LIMIT_BYTES = 62 * 1024 * 1024

GRID_W = 64
FOURIER_GROUPS = 4
CONV_W = 4
CONV_LEFT = (CONV_W - 1) // 2
GATE_C = 8.0
N_MOD = 9
EPS = 1e-6

BF16 = jnp.bfloat16
F32 = jnp.float32

SEG_LEN = 4
SCAN_ROWS = SEG_LEN * SUBLANES


def _sigmoid(x):
    return 0.5 * jnp.tanh(0.5 * x) + 0.5


def _silu(x):
    return x * _sigmoid(x)


def _gelu_tanh(x):
    c = math.sqrt(2.0 / math.pi)
    return 0.5 * x * (1.0 + jnp.tanh(c * (x + 0.044715 * (x * x * x))))


def _rms(x, g):
    return x * lax.rsqrt(jnp.mean(x * x, axis=-1, keepdims=True) + EPS) * g


def _resident(shape):
    nd = len(shape)
    return pl.BlockSpec(shape, lambda *_: (0,) * nd, pipeline_mode=pl.Buffered(1))


def _params(semantics, vmem=VMEM_LIMIT_BYTES):
    return pltpu.CompilerParams(dimension_semantics=semantics, vmem_limit_bytes=vmem)


def _ada_kernel(c_ref, w_ref, b_ref, o_ref):
    s = _silu(c_ref[...]).astype(BF16)
    o_ref[...] = jnp.dot(s, w_ref[...].astype(BF16), preferred_element_type=F32) + b_ref[...]


def _ada(cc, w, b, tn=1536):
    m, d = cc.shape
    n = w.shape[1]
    return pl.pallas_call(
        _ada_kernel,
        grid=(n // tn,),
        in_specs=[pl.BlockSpec((m, d), lambda j: (0, 0)),
                  pl.BlockSpec((d, tn), lambda j: (0, j)),
                  pl.BlockSpec((1, tn), lambda j: (0, j))],
        out_specs=pl.BlockSpec((m, tn), lambda j: (0, j)),
        out_shape=jax.ShapeDtypeStruct((m, n), F32),
        compiler_params=_params(("arbitrary",)),
        name="adaln",
    )(cc, w, b)


def _ffn_body(x, sh, sc, ga, gpre, gpost, wi_ref, wo_ref, act_ref, chunk, between=None):
    dff = wo_ref.shape[0]
    h = (_rms(x, gpre) * (1.0 + sc) + sh).astype(BF16)
    for i, c in enumerate(range(0, dff, chunk)):
        gate = jnp.dot(h, wi_ref[:, c:c + chunk], preferred_element_type=F32)
        up = jnp.dot(h, wi_ref[:, dff + c:dff + c + chunk], preferred_element_type=F32)
        act_ref[:, c:c + chunk] = (_silu(gate) * up).astype(BF16)
        if between is not None:
            between(i, dff // chunk)
    y = jnp.dot(act_ref[...], wo_ref[...], preferred_element_type=F32)
    return x + (0.5 * ga) * _rms(y, gpost)


def _ffn_kernel(x_ref, sh_ref, sc_ref, ga_ref, gpre_ref, gpost_ref, wi_ref, wo_ref,
                o_ref, act_ref, *, chunk):
    o_ref[0] = _ffn_body(x_ref[0], sh_ref[0], sc_ref[0], ga_ref[0], gpre_ref[...], gpost_ref[...],
                         wi_ref, wo_ref, act_ref, chunk)


def _ffn(x, mod, mod_row, k_shift, g_pre, g_post, wi, wo, tm=512, chunk=MXU_DIM):
    bsz, seq, d = x.shape
    dff = wo.shape[0]
    mod_spec = lambda k: pl.BlockSpec((1, 1, d), lambda b, i: (mod_row(b), 0, k))
    return pl.pallas_call(
        functools.partial(_ffn_kernel, chunk=chunk),
        grid=(bsz, seq // tm),
        in_specs=[pl.BlockSpec((1, tm, d), lambda b, i: (b, i, 0)),
                  mod_spec(k_shift), mod_spec(k_shift + 1), mod_spec(k_shift + 2),
                  _resident((1, d)), _resident((1, d)),
                  _resident(wi.shape), _resident(wo.shape)],
        out_specs=pl.BlockSpec((1, tm, d), lambda b, i: (b, i, 0)),
        out_shape=jax.ShapeDtypeStruct(x.shape, F32),
        scratch_shapes=[pltpu.VMEM((tm, dff), BF16)],
        compiler_params=_params(("arbitrary", "arbitrary")),
        name="ffn",
    )(x, mod, mod, mod, g_pre, g_post, wi, wo)


def _inproj_kernel(x_ref, sh_ref, sc_ref, g_ref, w_ref, *o_refs, splits, acts, chunk):
    h = (_rms(x_ref[0], g_ref[...]) * (1.0 + sc_ref[0]) + sh_ref[0]).astype(BF16)
    for o_ref, (start, width), act in zip(o_refs, splits, acts):
        for c in range(0, width, chunk):
            r = jnp.dot(h, w_ref[:, start + c:start + c + chunk], preferred_element_type=F32)
            if act is not None:
                r = act(r)
            o_ref[0, :, c:c + chunk] = r.astype(o_ref.dtype)


def _inproj(x, mod, mod_row, k_shift, g, w, splits, dtypes, acts, tm=512, chunk=512):
    bsz, seq, d = x.shape
    mod_spec = lambda k: pl.BlockSpec((1, 1, d), lambda b, i: (mod_row(b), 0, k))
    return pl.pallas_call(
        functools.partial(_inproj_kernel, splits=splits, acts=acts, chunk=chunk),
        grid=(bsz, seq // tm),
        in_specs=[pl.BlockSpec((1, tm, d), lambda b, i: (b, i, 0)),
                  mod_spec(k_shift), mod_spec(k_shift + 1),
                  _resident((1, d)), _resident(w.shape)],
        out_specs=[pl.BlockSpec((1, tm, wd), lambda b, i: (b, i, 0)) for _, wd in splits],
        out_shape=[jax.ShapeDtypeStruct((bsz, seq, wd), dt) for (_, wd), dt in zip(splits, dtypes)],
        compiler_params=_params(("arbitrary", "arbitrary")),
        name="inproj",
    )(x, mod, mod, g, w)


def _lru_c1(lam):
    e = jnp.exp(-jnp.abs(lam))
    u = 1.0 + e
    log1p_e = jnp.where(u == 1.0, e, jnp.log(u) * (e / (u - 1.0)))
    return (-0.5 * GATE_C * math.log2(math.e)) * (jnp.maximum(-lam, 0.0) + log1p_e)


def _lru_fill_ext(ext_ref, j, prev, main, nxt, at_start, at_end):
    tt = main.shape[0]
    ext_ref[j, 0:SUBLANES, :] = jnp.where(at_start, 0.0, prev)
    ext_ref[j, SUBLANES:SUBLANES + tt, :] = main
    ext_ref[j, SUBLANES + tt:2 * SUBLANES + tt, :] = jnp.where(at_end, 0.0, nxt)


def _lru_gates(ext_ref, a_ref, b_ref, j, row0, nrows, cw, cb, wg, br, bi, c1):
    xc = cb
    for k in range(CONV_W):
        off = SUBLANES - CONV_LEFT + k + row0
        tap = jnp.concatenate(
            [ext_ref[j, pl.ds(off + m + kk, SUBLANES, stride=SEG_LEN), :]
             for m in range(0, nrows, SCAN_ROWS) for kk in range(SEG_LEN)], axis=0)
        xc = xc + tap * cw[k:k + 1, :]
    z = jnp.dot(xc.astype(BF16), wg, preferred_element_type=F32)
    t_r = jnp.tanh(z[:, :LANES] + br)
    t_i = jnp.tanh(z[:, LANES:] + bi)
    a = jnp.exp2(c1 * t_r + c1)
    y = 1.0 - a * a
    root = jnp.where(y > 0.0, y * lax.rsqrt(y), 0.0)
    hx = 0.5 * xc
    a_ref[j] = a
    b_ref[j] = root * (t_i * hx + hx)


def _shift_rows(x, k, fill, rows):
    return jnp.where(rows >= k, pltpu.roll(x, k, axis=0), fill)


def _shift_rows_up(x, k, fill, rows):
    return jnp.where(rows < SUBLANES - k, pltpu.roll(x, SUBLANES - k, axis=0), fill)


def _scan_group(a_ref, b_ref, o_ref, j, base, out_base, carry, rows, reverse):
    ks = range(SEG_LEN - 1, -1, -1) if reverse else range(SEG_LEN)
    shift = _shift_rows_up if reverse else _shift_rows
    a = {k: a_ref[j, pl.ds(base + k * SUBLANES, SUBLANES), :] for k in ks}
    b = {k: b_ref[j, pl.ds(base + k * SUBLANES, SUBLANES), :] for k in ks}
    h, p = {}, {}
    prev = None
    for k in ks:
        if prev is None:
            h[k], p[k] = b[k], a[k]
        else:
            h[k] = a[k] * h[prev] + b[k]
            p[k] = a[k] * p[prev]
        prev = k
    hc, pc = h[prev], p[prev]
    for dd in (1, 2, 4):
        hc = pc * shift(hc, dd, 0.0, rows) + hc
        pc = pc * shift(pc, dd, 1.0, rows)
    g = hc + pc * carry
    cin = shift(g, 1, carry, rows)
    for k in ks:
        o_ref[j, pl.ds(out_base + k, SUBLANES, stride=SEG_LEN), :] = h[k] + p[k] * cin
    last = 0 if reverse else SUBLANES - 1
    return jnp.broadcast_to(g[last:last + 1, :], (SUBLANES, LANES))


def _rec_kernel(l_ref, lp_ref, ln_ref, cw_ref, cb_ref, wg_ref, br_ref, bi_ref, lam_ref, h0_ref,
                o_ref, ext_ref, a_ref, b_ref, ho_ref, carry_ref, *, tt, nc):
    d = pl.program_id(0)
    c = pl.program_id(2)
    tc = c + d * (nc - 1 - 2 * c)
    n_slab = a_ref.shape[0]
    c1 = _lru_c1(lam_ref[0])

    for j in range(n_slab):
        sl = slice(j * LANES, (j + 1) * LANES)
        _lru_fill_ext(ext_ref, j, lp_ref[0, :, sl], l_ref[0, :, sl], ln_ref[0, :, sl], tc == 0, tc == nc - 1)
        _lru_gates(ext_ref, a_ref, b_ref, j, 0, tt, cw_ref[:, sl], cb_ref[:, sl], wg_ref[0, j],
                   br_ref[0, :, sl], bi_ref[0, :, sl], c1[:, sl])

    @pl.when(c == 0)
    def _():
        for j in range(n_slab):
            carry_ref[j] = jnp.broadcast_to(h0_ref[0, 0, :, j * LANES:(j + 1) * LANES], (SUBLANES, LANES))

    rows = lax.broadcasted_iota(jnp.int32, (SUBLANES, LANES), 0)
    n_groups = tt // SCAN_ROWS

    def run(reverse):
        def body(m, carries):
            base = pl.multiple_of(((n_groups - 1 - m) if reverse else m) * SCAN_ROWS, SCAN_ROWS)
            return tuple(_scan_group(a_ref, b_ref, ho_ref, j, base, base, carries[j], rows, reverse)
                         for j in range(n_slab))
        carries = lax.fori_loop(0, n_groups, body, tuple(carry_ref[j] for j in range(n_slab)))
        for j in range(n_slab):
            carry_ref[j] = carries[j]

    @pl.when(d == 0)
    def _():
        run(False)

    @pl.when(d == 1)
    def _():
        run(True)

    for j in range(n_slab):
        o_ref[0, 0, :, j * LANES:(j + 1) * LANES] = ho_ref[j].astype(o_ref.dtype)


def _halo_specs(tt, t, ch, idx):
    hb = tt // SUBLANES
    last = t // SUBLANES - 1

    def prev_map(*g):
        b, tc = idx(*g)
        return b, jnp.maximum(tc * hb - 1, 0), 0

    def next_map(*g):
        b, tc = idx(*g)
        return b, jnp.minimum((tc + 1) * hb, last), 0

    return [pl.BlockSpec((1, tt, ch), lambda *g: idx(*g) + (0,)),
            pl.BlockSpec((1, SUBLANES, ch), prev_map),
            pl.BlockSpec((1, SUBLANES, ch), next_map)]


def _rglru(l, conv_w, conv_b, wg, br, bi, lam, h0, n_dir, tt=256):
    bsz, t, ch = l.shape
    nc = t // tt
    n_slab = ch // LANES
    tcf = lambda d, c: c + d * (nc - 1 - 2 * c)
    vec = lambda: pl.BlockSpec((1, 1, ch), lambda d, b, c: (d, 0, 0))
    return pl.pallas_call(
        functools.partial(_rec_kernel, tt=tt, nc=nc),
        grid=(n_dir, bsz, nc),
        in_specs=_halo_specs(tt, t, ch, lambda d, b, c: (b, tcf(d, c))) + [
            pl.BlockSpec(conv_w.shape, lambda d, b, c: (0, 0)),
            pl.BlockSpec((1, ch), lambda d, b, c: (0, 0)),
            pl.BlockSpec((1,) + wg.shape[1:], lambda d, b, c: (d, 0, 0, 0)),
            vec(), vec(), vec(),
            pl.BlockSpec((1, 1, 1, ch), lambda d, b, c: (d, b, 0, 0))],
        out_specs=pl.BlockSpec((1, 1, tt, ch), lambda d, b, c: (d, b, tcf(d, c), 0)),
        out_shape=jax.ShapeDtypeStruct((n_dir, bsz, t, ch), F32),
        scratch_shapes=[pltpu.VMEM((n_slab, tt + 2 * SUBLANES, LANES), F32),
                        pltpu.VMEM((n_slab, tt, LANES), F32),
                        pltpu.VMEM((n_slab, tt, LANES), F32),
                        pltpu.VMEM((n_slab, tt, LANES), F32),
                        pltpu.VMEM((n_slab, SUBLANES, LANES), F32)],
        compiler_params=_params(("arbitrary", "arbitrary", "arbitrary")),
        name="rglru",
    )(l, l, l, conv_w, conv_b, wg, br, bi, lam, h0)


def _dft_tables(rows, cols, group_w, groups):
    n_tok = rows * cols
    scale = 1.0 / math.sqrt(n_tok * group_w)
    ch = np.arange(group_w)
    ph = 2.0 * np.pi * ((ch[:, None] * ch[None, :]) % group_w) / group_w
    d_model = groups * group_w
    dc = np.zeros((d_model, 2 * d_model), np.float64)
    for gi in range(groups):
        s = slice(gi * group_w, (gi + 1) * group_w)
        dc[s, gi * group_w:(gi + 1) * group_w] = np.cos(ph)
        dc[s, d_model + gi * group_w:d_model + (gi + 1) * group_w] = np.sin(ph)
    t = np.arange(n_tok)
    r, c = t // cols, t % cols
    th = 2.0 * np.pi * (((r[:, None] * r[None, :]) % rows) / rows + ((c[:, None] * c[None, :]) % cols) / cols)
    wc = np.cos(th) * scale
    ws = -np.sin(th) * scale
    as_bf16 = lambda m: jnp.asarray(m.astype(np.float32)).astype(BF16)
    return as_bf16(dc), as_bf16(wc), as_bf16(ws)


def _fourier_kernel(f_ref, dc_ref, wc_ref, ws_ref, o_ref, z_ref, *, tm):
    n_tok, df = f_ref.shape[1], f_ref.shape[2]
    for m in range(0, n_tok, tm):
        z = jnp.dot(f_ref[0, m:m + tm, :], dc_ref[...], preferred_element_type=F32)
        z_ref[m:m + tm, :] = z.astype(BF16)
    for m in range(0, n_tok, tm):
        o = jnp.dot(wc_ref[m:m + tm, :], z_ref[:, :df], preferred_element_type=F32)
        o = o + jnp.dot(ws_ref[m:m + tm, :], z_ref[:, df:], preferred_element_type=F32)
        o_ref[0, m:m + tm, :] = o.astype(o_ref.dtype)


def _fourier(f, dc, wc, ws, tm=512):
    bsz, n_tok, df = f.shape
    return pl.pallas_call(
        functools.partial(_fourier_kernel, tm=tm),
        grid=(bsz,),
        in_specs=[pl.BlockSpec((1, n_tok, df), lambda b: (b, 0, 0)),
                  _resident(dc.shape), _resident(wc.shape), _resident(ws.shape)],
        out_specs=pl.BlockSpec((1, n_tok, df), lambda b: (b, 0, 0)),
        out_shape=jax.ShapeDtypeStruct(f.shape, BF16),
        scratch_shapes=[pltpu.VMEM((n_tok, 2 * df), BF16)],
        compiler_params=_params(("arbitrary",)),
        name="fourier",
    )(f, dc, wc, ws)


def _tail_kernel(l_ref, lp_ref, ln_ref, cw_ref, cb_ref, wg_ref, br_ref, bi_ref, lam_ref, h0_ref,
                 x_ref, hf_ref, g_ref, ga_ref, gb_ref, fo_ref,
                 gate2_ref, sh3_ref, sc3_ref, gate3_ref, gn_ref, gpre_ref, gpost_ref,
                 wfa_ref, wfb_ref, wo_ref, wi2_ref, wo2_ref,
                 o_ref, ext_ref, a_ref, b_ref, hb_ref, carry_ref, act_ref, *, tm, nt, n_tiles, half, chunk):
    s = pl.program_id(0)
    q = jnp.minimum(s, n_tiles - 1)
    cq = nt - 1 - q % nt
    n_slab = a_ref.shape[0]

    @pl.when(s == 0)
    def _():
        hb_ref[...] = jnp.zeros(hb_ref.shape, F32)
        carry_ref[...] = jnp.zeros(carry_ref.shape, F32)

    hb = jnp.concatenate([hb_ref[j] for j in range(n_slab)], axis=1)
    u = ((hf_ref[0, 0] + hb) * g_ref[0].astype(F32)).astype(BF16)
    yb = jnp.dot(u, wfb_ref[...], preferred_element_type=F32)
    ya = jnp.dot(fo_ref[0], wfa_ref[...], preferred_element_type=F32)
    m = ga_ref[0].astype(F32) * ya + gb_ref[0].astype(F32) * yb
    mx = jnp.dot(m.astype(BF16), wo_ref[...], preferred_element_type=F32)
    x2 = x_ref[0] + gate2_ref[0] * _rms(mx, gn_ref[...])

    c1 = _lru_c1(lam_ref[...])
    rows = lax.broadcasted_iota(jnp.int32, (SUBLANES, LANES), 0)
    carries = []
    for j in range(n_slab):
        sl = slice(j * LANES, (j + 1) * LANES)
        _lru_fill_ext(ext_ref, j, lp_ref[0, :, sl], l_ref[0, :, sl], ln_ref[0, :, sl], cq == 0, cq == nt - 1)
        h0 = jnp.broadcast_to(h0_ref[0, :, sl], (SUBLANES, LANES))
        carries.append(jnp.where(cq == nt - 1, h0, carry_ref[j]))

    def gates_piece(row0, j):
        sl = slice(j * LANES, (j + 1) * LANES)
        _lru_gates(ext_ref, a_ref, b_ref, j, row0, half, cw_ref[:, sl], cb_ref[:, sl], wg_ref[j],
                   br_ref[:, sl], bi_ref[:, sl], c1[:, sl])

    def scan_piece(row0, base):
        carries[:] = [_scan_group(a_ref, b_ref, hb_ref, j, base, row0 + base, carries[j], rows, True)
                      for j in range(n_slab)]

    pieces = []
    for row0 in range(tm - half, -1, -half):
        pieces += [functools.partial(gates_piece, row0, j) for j in range(n_slab)]
        pieces += [functools.partial(scan_piece, row0, base) for base in range(half - SCAN_ROWS, -1, -SCAN_ROWS)]

    def between(i, n):
        for piece in pieces[i * len(pieces) // n:(i + 1) * len(pieces) // n]:
            piece()

    o_ref[0] = _ffn_body(x2, sh3_ref[0], sc3_ref[0], gate3_ref[0], gpre_ref[...], gpost_ref[...],
                         wi2_ref, wo2_ref, act_ref, chunk, between)
    for j in range(n_slab):
        carry_ref[j] = carries[j]


def _tail(l, conv_w, conv_b, wg, br, bi, lam, h0, x, hf, g, ga, gb, fo, mod, gn, gpre, gpost,
          wfa, wfb, wo, wi2, wo2, tm=512, half=256, chunk=MXU_DIM):
    bsz, seq, d = x.shape
    ch = l.shape[2]
    nt = seq // tm
    n_tiles = bsz * nt
    n_slab = ch // LANES
    dff = wo2.shape[0]

    def scan_tile(s):
        q = jnp.minimum(s, n_tiles - 1)
        return q // nt, nt - 1 - q % nt

    def merge_tile(s):
        r = jnp.maximum(s - 1, 0)
        return r // nt, nt - 1 - r % nt

    tok = lambda w: pl.BlockSpec((1, tm, w), lambda s: merge_tile(s) + (0,))
    mod_spec = lambda k: pl.BlockSpec((1, 1, d), lambda s: (merge_tile(s)[0], 0, k))
    return pl.pallas_call(
        functools.partial(_tail_kernel, tm=tm, nt=nt, n_tiles=n_tiles, half=half, chunk=chunk),
        grid=(n_tiles + 1,),
        in_specs=_halo_specs(tm, seq, ch, scan_tile) + [
            _resident(conv_w.shape), _resident((1, ch)), _resident(wg.shape),
            _resident((1, ch)), _resident((1, ch)), _resident((1, ch)),
            pl.BlockSpec((1, 1, ch), lambda s: (scan_tile(s)[0], 0, 0)),
            tok(d),
            pl.BlockSpec((1, 1, tm, ch), lambda s: (0,) + merge_tile(s) + (0,)),
            tok(ch), tok(d), tok(d), tok(fo.shape[2]),
            mod_spec(5), mod_spec(6), mod_spec(7), mod_spec(8),
            _resident((1, d)), _resident((1, d)), _resident((1, d)),
            _resident(wfa.shape), _resident(wfb.shape), _resident(wo.shape),
            _resident(wi2.shape), _resident(wo2.shape)],
        out_specs=tok(d),
        out_shape=jax.ShapeDtypeStruct(x.shape, F32),
        scratch_shapes=[pltpu.VMEM((n_slab, tm + 2 * SUBLANES, LANES), F32),
                        pltpu.VMEM((n_slab, half, LANES), F32),
                        pltpu.VMEM((n_slab, half, LANES), F32),
                        pltpu.VMEM((n_slab, tm, LANES), F32),
                        pltpu.VMEM((n_slab, SUBLANES, LANES), F32),
                        pltpu.VMEM((tm, dff), BF16)],
        compiler_params=_params(("arbitrary",), TAIL_VMEM_LIMIT_BYTES),
        name="tail",
    )(l, l, l, conv_w, conv_b, wg, br, bi, lam, h0,
      x, hf, g, ga, gb, fo, mod, mod, mod, mod, gn, gpre, gpost, wfa, wfb, wo, wi2, wo2)


def kernel(x, c, ctx, c_ctx, w_ada, b_ada, norm_g, w_ffn1_in, w_ffn1_out, w_ffn2_in, w_ffn2_out,
           w_in, conv_w, conv_b, w_r, b_r, w_i, b_i, lam, w_fa, w_fb, w_out):
    depth = w_ada.shape[0]
    assert depth == 1, "single-layer problem: the context stream is only needed up to the mixer scans"
    bsz, seq, d = x.shape
    d_lru = w_fb.shape[1]
    d_f = w_fa.shape[1]
    lyr = 0

    pad = (-(bsz + 1)) % SUBLANES
    cc = jnp.concatenate([c, c_ctx[None, :], jnp.zeros((pad, d), F32)], axis=0)
    mod = _ada(cc, w_ada[lyr], b_ada[lyr][None, :])
    mod = mod.reshape(mod.shape[0], 1, N_MOD * d)
    lat_row = lambda b: b
    ctx_row = lambda b: bsz
    g = norm_g[lyr][:, None, :]

    wi1, wo1 = w_ffn1_in[lyr].astype(BF16), w_ffn1_out[lyr].astype(BF16)
    wi2, wo2 = w_ffn2_in[lyr].astype(BF16), w_ffn2_out[lyr].astype(BF16)
    w_in_b = w_in[lyr].astype(BF16)

    x1 = _ffn(x, mod, lat_row, 0, g[0], g[1], wi1, wo1)
    ctx_flat = ctx.reshape(1, bsz * ctx.shape[1], d)
    c1 = _ffn(ctx_flat, mod, ctx_row, 0, g[0], g[1], wi1, wo1)

    splits = ((0, d_f), (d_f, d_lru), (d_f + d_lru, d_lru), (d_f + 2 * d_lru, d), (d_f + 2 * d_lru + d, d))
    fx, lx, gx, gax, gbx = _inproj(x1, mod, lat_row, 3, g[2], w_in_b, splits, (BF16, F32, BF16, BF16, BF16),
                                   (None, None, _gelu_tanh, _sigmoid, _sigmoid))
    (lc,) = _inproj(c1, mod, ctx_row, 3, g[2], w_in_b[:, d_f:d_f + d_lru], ((0, d_lru),), (F32,), (None,))
    lc = lc.reshape(bsz, ctx.shape[1], d_lru)

    wgate = (0.5 * jnp.concatenate([w_r[lyr], w_i[lyr]], axis=-1)).astype(BF16)
    br, bi, lm = 0.5 * b_r[lyr][:, None, :], 0.5 * b_i[lyr][:, None, :], lam[lyr][:, None, :]
    cw, cb = conv_w[lyr], conv_b[lyr][None, :]
    h_ctx = _rglru(lc, cw, cb, wgate, br, bi, lm, jnp.zeros((2, bsz, 1, d_lru), F32), 2, tt=ctx.shape[1])
    h0f = h_ctx[0, :, -1][None, :, None, :]
    h0b = h_ctx[1, :, 0][:, None, :]
    hf = _rglru(lx, cw, cb, wgate, br, bi, lm, h0f, 1)

    dc, wc, ws = _dft_tables(seq // GRID_W, GRID_W, d_f // FOURIER_GROUPS, FOURIER_GROUPS)
    fo = _fourier(fx, dc, wc, ws)

    return _tail(lx, cw, cb, wgate[1], br[1], bi[1], lm[1], h0b, x1, hf, gx, gax, gbx, fo, mod,
                 g[3], g[4], g[5], w_fa[lyr].astype(BF16), w_fb[lyr].astype(BF16), w_out[lyr].astype(BF16),
                 wi2, wo2)
```

```python
import functools
import math

import numpy as np
import jax
import jax.numpy as jnp
from jax import lax
from jax.experimental import pallas as pl
from jax.experimental.pallas import tpu as pltpu

LANES = 128
SUBLANES = 8
MXU_DIM = 256
VMEM_LIMIT_BYTES = 56 * 1024 * 1024
TAIL_VMEM_LIMIT_BYTES = 62 * 1024 * 1024

GRID_W = 64
FOURIER_GROUPS = 4
CONV_W = 4
CONV_LEFT = (CONV_W - 1) // 2
GATE_C = 8.0
N_MOD = 9
EPS = 1e-6

BF16 = jnp.bfloat16
F32 = jnp.float32

SEG_LEN = 4
SCAN_ROWS = SEG_LEN * SUBLANES


def _sigmoid(x):
    return 0.5 * jnp.tanh(0.5 * x) + 0.5


def _silu(x):
    return x * _sigmoid(x)


def _gelu_tanh(x):
    c = math.sqrt(2.0 / math.pi)
    return 0.5 * x * (1.0 + jnp.tanh(c * (x + 0.044715 * (x * x * x))))


def _rms(x, g):
    return x * lax.rsqrt(jnp.mean(x * x, axis=-1, keepdims=True) + EPS) * g


def _resident(shape):
    nd = len(shape)
    return pl.BlockSpec(shape, lambda *_: (0,) * nd, pipeline_mode=pl.Buffered(1))


def _params(semantics, vmem=VMEM_LIMIT_BYTES):
    return pltpu.CompilerParams(dimension_semantics=semantics, vmem_limit_bytes=vmem)


def _interleave(major, minor):
    merged, done = [], 0
    for i, piece in enumerate(major):
        while done < len(minor) and done * len(major) <= i * len(minor):
            merged.append(minor[done])
            done += 1
        merged.append(piece)
    return merged + list(minor[done:])


def _ada_kernel(c_ref, w_ref, b_ref, o_ref):
    s = _silu(c_ref[...]).astype(BF16)
    o_ref[...] = jnp.dot(s, w_ref[...].astype(BF16), preferred_element_type=F32) + b_ref[...]


def _ada(cc, w, b, tn=1536):
    m, d = cc.shape
    n = w.shape[1]
    return pl.pallas_call(
        _ada_kernel,
        grid=(n // tn,),
        in_specs=[pl.BlockSpec((m, d), lambda j: (0, 0)),
                  pl.BlockSpec((d, tn), lambda j: (0, j)),
                  pl.BlockSpec((1, tn), lambda j: (0, j))],
        out_specs=pl.BlockSpec((m, tn), lambda j: (0, j)),
        out_shape=jax.ShapeDtypeStruct((m, n), F32),
        compiler_params=_params(("arbitrary",)),
        name="adaln",
    )(cc, w, b)


def _ffn_body(x, sh, sc, ga, gpre, gpost, wi_ref, wo_ref, act_ref, chunk, between=None):
    dff = wo_ref.shape[0]
    h = (_rms(x, gpre) * (1.0 + sc) + sh).astype(BF16)
    for i, c in enumerate(range(0, dff, chunk)):
        gate = jnp.dot(h, wi_ref[:, c:c + chunk], preferred_element_type=F32)
        up = jnp.dot(h, wi_ref[:, dff + c:dff + c + chunk], preferred_element_type=F32)
        act_ref[:, c:c + chunk] = (_silu(gate) * up).astype(BF16)
        if between is not None:
            between(i, dff // chunk)
    y = jnp.dot(act_ref[...], wo_ref[...], preferred_element_type=F32)
    return x + (0.5 * ga) * _rms(y, gpost)


def _ffn_kernel(x_ref, sh_ref, sc_ref, ga_ref, gpre_ref, gpost_ref, wi_ref, wo_ref,
                o_ref, act_ref, *, chunk):
    o_ref[0] = _ffn_body(x_ref[0], sh_ref[0], sc_ref[0], ga_ref[0], gpre_ref[...], gpost_ref[...],
                         wi_ref, wo_ref, act_ref, chunk)


def _ffn(x, mod, mod_row, k_shift, g_pre, g_post, wi, wo, tm=512, chunk=MXU_DIM):
    bsz, seq, d = x.shape
    dff = wo.shape[0]
    mod_spec = lambda k: pl.BlockSpec((1, 1, d), lambda b, i: (mod_row(b), 0, k))
    return pl.pallas_call(
        functools.partial(_ffn_kernel, chunk=chunk),
        grid=(bsz, seq // tm),
        in_specs=[pl.BlockSpec((1, tm, d), lambda b, i: (b, i, 0)),
                  mod_spec(k_shift), mod_spec(k_shift + 1), mod_spec(k_shift + 2),
                  _resident((1, d)), _resident((1, d)),
                  _resident(wi.shape), _resident(wo.shape)],
        out_specs=pl.BlockSpec((1, tm, d), lambda b, i: (b, i, 0)),
        out_shape=jax.ShapeDtypeStruct(x.shape, F32),
        scratch_shapes=[pltpu.VMEM((tm, dff), BF16)],
        compiler_params=_params(("arbitrary", "arbitrary")),
        name="ffn",
    )(x, mod, mod, mod, g_pre, g_post, wi, wo)


def _inproj_kernel(x_ref, sh_ref, sc_ref, g_ref, w_ref, *o_refs, splits, acts, chunk):
    h = (_rms(x_ref[0], g_ref[...]) * (1.0 + sc_ref[0]) + sh_ref[0]).astype(BF16)
    for o_ref, (start, width), act in zip(o_refs, splits, acts):
        for c in range(0, width, chunk):
            r = jnp.dot(h, w_ref[:, start + c:start + c + chunk], preferred_element_type=F32)
            if act is not None:
                r = act(r)
            o_ref[0, :, c:c + chunk] = r.astype(o_ref.dtype)


def _inproj(x, mod, mod_row, k_shift, g, w, splits, dtypes, acts, tm=512, chunk=512):
    bsz, seq, d = x.shape
    mod_spec = lambda k: pl.BlockSpec((1, 1, d), lambda b, i: (mod_row(b), 0, k))
    return pl.pallas_call(
        functools.partial(_inproj_kernel, splits=splits, acts=acts, chunk=chunk),
        grid=(bsz, seq // tm),
        in_specs=[pl.BlockSpec((1, tm, d), lambda b, i: (b, i, 0)),
                  mod_spec(k_shift), mod_spec(k_shift + 1),
                  _resident((1, d)), _resident(w.shape)],
        out_specs=[pl.BlockSpec((1, tm, wd), lambda b, i: (b, i, 0)) for _, wd in splits],
        out_shape=[jax.ShapeDtypeStruct((bsz, seq, wd), dt) for (_, wd), dt in zip(splits, dtypes)],
        compiler_params=_params(("arbitrary", "arbitrary")),
        name="inproj",
    )(x, mod, mod, g, w)


def _lru_c1(lam):
    e = jnp.exp(-jnp.abs(lam))
    u = 1.0 + e
    log1p_e = jnp.where(u == 1.0, e, jnp.log(u) * (e / (u - 1.0)))
    return (-0.5 * GATE_C * math.log2(math.e)) * (jnp.maximum(-lam, 0.0) + log1p_e)


def _lru_fill_ext(ext_ref, j, prev, main, nxt, at_start, at_end):
    tt = main.shape[0]
    ext_ref[j, 0:SUBLANES, :] = jnp.where(at_start, 0.0, prev)
    ext_ref[j, SUBLANES:SUBLANES + tt, :] = main
    ext_ref[j, SUBLANES + tt:2 * SUBLANES + tt, :] = jnp.where(at_end, 0.0, nxt)


def _lru_gates(ext_ref, a_ref, b_ref, j, row0, nrows, cw, cb, wg, br, bi, c1):
    xc, z = _lru_conv_z(ext_ref, j, row0, nrows, cw, cb, wg)
    _lru_gate_math(a_ref, b_ref, j, xc, z, br, bi, c1)


def _lru_conv_z(ext_ref, j, row0, nrows, cw, cb, wg):
    xc = cb
    for k in range(CONV_W):
        off = SUBLANES - CONV_LEFT + k + row0
        tap = jnp.concatenate(
            [ext_ref[j, pl.ds(off + m + kk, SUBLANES, stride=SEG_LEN), :]
             for m in range(0, nrows, SCAN_ROWS) for kk in range(SEG_LEN)], axis=0)
        xc = xc + tap * cw[k:k + 1, :]
    return xc, jnp.dot(xc.astype(BF16), wg, preferred_element_type=F32)


def _lru_gate_math(a_ref, b_ref, j, xc, z, br, bi, c1):
    t_r = jnp.tanh(z[:, :LANES] + br)
    t_i = jnp.tanh(z[:, LANES:] + bi)
    a = jnp.exp2(c1 * t_r + c1)
    y = 1.0 - a * a
    root = jnp.where(y > 0.0, y * lax.rsqrt(y), 0.0)
    hx = 0.5 * xc
    a_ref[j] = a
    b_ref[j] = root * (t_i * hx + hx)


def _shift_rows(x, k, fill, rows):
    return jnp.where(rows >= k, pltpu.roll(x, k, axis=0), fill)


def _shift_rows_up(x, k, fill, rows):
    return jnp.where(rows < SUBLANES - k, pltpu.roll(x, SUBLANES - k, axis=0), fill)


def _scan_group(a_ref, b_ref, o_ref, j, base, out_base, carry, rows, reverse):
    ks = range(SEG_LEN - 1, -1, -1) if reverse else range(SEG_LEN)
    shift = _shift_rows_up if reverse else _shift_rows
    a = {k: a_ref[j, pl.ds(base + k * SUBLANES, SUBLANES), :] for k in ks}
    b = {k: b_ref[j, pl.ds(base + k * SUBLANES, SUBLANES), :] for k in ks}
    h, p = {}, {}
    prev = None
    for k in ks:
        if prev is None:
            h[k], p[k] = b[k], a[k]
        else:
            h[k] = a[k] * h[prev] + b[k]
            p[k] = a[k] * p[prev]
        prev = k
    hc, pc = h[prev], p[prev]
    for dd in (1, 2, 4):
        hc = pc * shift(hc, dd, 0.0, rows) + hc
        pc = pc * shift(pc, dd, 1.0, rows)
    g = hc + pc * carry
    cin = shift(g, 1, carry, rows)
    for k in ks:
        o_ref[j, pl.ds(out_base + k, SUBLANES, stride=SEG_LEN), :] = h[k] + p[k] * cin
    last = 0 if reverse else SUBLANES - 1
    return jnp.broadcast_to(g[last:last + 1, :], (SUBLANES, LANES))


def _rec_kernel(l_ref, lp_ref, ln_ref, cw_ref, cb_ref, wg_ref, br_ref, bi_ref, lam_ref, h0_ref,
                o_ref, ext_ref, a_ref, b_ref, ho_ref, carry_ref, *, tt, nc):
    d = pl.program_id(0)
    c = pl.program_id(2)
    tc = c + d * (nc - 1 - 2 * c)
    n_slab = a_ref.shape[0]
    c1 = _lru_c1(lam_ref[0])

    for j in range(n_slab):
        sl = slice(j * LANES, (j + 1) * LANES)
        _lru_fill_ext(ext_ref, j, lp_ref[0, :, sl], l_ref[0, :, sl], ln_ref[0, :, sl], tc == 0, tc == nc - 1)
        _lru_gates(ext_ref, a_ref, b_ref, j, 0, tt, cw_ref[:, sl], cb_ref[:, sl], wg_ref[0, j],
                   br_ref[0, :, sl], bi_ref[0, :, sl], c1[:, sl])

    @pl.when(c == 0)
    def _():
        for j in range(n_slab):
            carry_ref[j] = jnp.broadcast_to(h0_ref[0, 0, :, j * LANES:(j + 1) * LANES], (SUBLANES, LANES))

    rows = lax.broadcasted_iota(jnp.int32, (SUBLANES, LANES), 0)
    n_groups = tt // SCAN_ROWS

    def run(reverse):
        def body(m, carries):
            base = pl.multiple_of(((n_groups - 1 - m) if reverse else m) * SCAN_ROWS, SCAN_ROWS)
            return tuple(_scan_group(a_ref, b_ref, ho_ref, j, base, base, carries[j], rows, reverse)
                         for j in range(n_slab))
        carries = lax.fori_loop(0, n_groups, body, tuple(carry_ref[j] for j in range(n_slab)))
        for j in range(n_slab):
            carry_ref[j] = carries[j]

    @pl.when(d == 0)
    def _():
        run(False)

    @pl.when(d == 1)
    def _():
        run(True)

    for j in range(n_slab):
        o_ref[0, 0, :, j * LANES:(j + 1) * LANES] = ho_ref[j].astype(o_ref.dtype)


def _halo_specs(tt, t, ch, idx):
    hb = tt // SUBLANES
    last = t // SUBLANES - 1

    def prev_map(*g):
        b, tc = idx(*g)
        return b, jnp.maximum(tc * hb - 1, 0), 0

    def next_map(*g):
        b, tc = idx(*g)
        return b, jnp.minimum((tc + 1) * hb, last), 0

    return [pl.BlockSpec((1, tt, ch), lambda *g: idx(*g) + (0,)),
            pl.BlockSpec((1, SUBLANES, ch), prev_map),
            pl.BlockSpec((1, SUBLANES, ch), next_map)]


def _rglru(l, conv_w, conv_b, wg, br, bi, lam, h0, n_dir, tt=256):
    bsz, t, ch = l.shape
    nc = t // tt
    n_slab = ch // LANES
    tcf = lambda d, c: c + d * (nc - 1 - 2 * c)
    vec = lambda: pl.BlockSpec((1, 1, ch), lambda d, b, c: (d, 0, 0))
    return pl.pallas_call(
        functools.partial(_rec_kernel, tt=tt, nc=nc),
        grid=(n_dir, bsz, nc),
        in_specs=_halo_specs(tt, t, ch, lambda d, b, c: (b, tcf(d, c))) + [
            pl.BlockSpec(conv_w.shape, lambda d, b, c: (0, 0)),
            pl.BlockSpec((1, ch), lambda d, b, c: (0, 0)),
            pl.BlockSpec((1,) + wg.shape[1:], lambda d, b, c: (d, 0, 0, 0)),
            vec(), vec(), vec(),
            pl.BlockSpec((1, 1, 1, ch), lambda d, b, c: (d, b, 0, 0))],
        out_specs=pl.BlockSpec((1, 1, tt, ch), lambda d, b, c: (d, b, tcf(d, c), 0)),
        out_shape=jax.ShapeDtypeStruct((n_dir, bsz, t, ch), F32),
        scratch_shapes=[pltpu.VMEM((n_slab, tt + 2 * SUBLANES, LANES), F32),
                        pltpu.VMEM((n_slab, tt, LANES), F32),
                        pltpu.VMEM((n_slab, tt, LANES), F32),
                        pltpu.VMEM((n_slab, tt, LANES), F32),
                        pltpu.VMEM((n_slab, SUBLANES, LANES), F32)],
        compiler_params=_params(("arbitrary", "arbitrary", "arbitrary")),
        name="rglru",
    )(l, l, l, conv_w, conv_b, wg, br, bi, lam, h0)


def _dft_tables(rows, cols, group_w, groups):
    n_tok = rows * cols
    scale = 1.0 / math.sqrt(n_tok * group_w)
    ch = np.arange(group_w)
    ph = 2.0 * np.pi * ((ch[:, None] * ch[None, :]) % group_w) / group_w
    d_f = groups * group_w
    d2 = np.zeros((2 * d_f, d_f), np.float64)
    for gi in range(groups):
        s = slice(gi * group_w, (gi + 1) * group_w)
        d2[s, s] = np.cos(ph)
        d2[d_f + gi * group_w:d_f + (gi + 1) * group_w, s] = np.sin(ph)
    t = np.arange(n_tok)
    r, c = t // cols, t % cols
    th = 2.0 * np.pi * (((r[:, None] * r[None, :]) % rows) / rows + ((c[:, None] * c[None, :]) % cols) / cols)
    wc = np.cos(th) * scale
    ws = -np.sin(th) * scale
    as_bf16 = lambda m: jnp.asarray(m.astype(np.float32)).astype(BF16)
    return as_bf16(wc), as_bf16(ws), as_bf16(d2)


def _fwd_kernel(l_ref, lp_ref, ln_ref, cw_ref, cb_ref, wg_ref, br_ref, bi_ref, lam_ref, h0_ref,
                f_ref, wc_ref, ws_ref, d2_ref,
                hf_ref, fo_ref, ext_ref, a_ref, b_ref, ho_ref, carry_ref, pq_ref,
                *, tt, nc, sub, n_split, k_split):
    c = pl.program_id(1)
    n_slab = a_ref.shape[0]
    df = f_ref.shape[2]

    @pl.when((pl.program_id(0) == 0) & (c == 0))
    def _():
        carry_ref[...] = jnp.zeros(carry_ref.shape, F32)

    c1 = _lru_c1(lam_ref[...])
    rows = lax.broadcasted_iota(jnp.int32, (SUBLANES, LANES), 0)
    carries = []
    for j in range(n_slab):
        sl = slice(j * LANES, (j + 1) * LANES)
        _lru_fill_ext(ext_ref, j, lp_ref[0, :, sl], l_ref[0, :, sl], ln_ref[0, :, sl], c == 0, c == nc - 1)
        h0 = jnp.broadcast_to(h0_ref[0, :, sl], (SUBLANES, LANES))
        carries.append(jnp.where(c == 0, h0, carry_ref[j]))

    def gates_piece(row0, j):
        sl = slice(j * LANES, (j + 1) * LANES)
        _lru_gates(ext_ref, a_ref.at[:, row0:row0 + sub], b_ref.at[:, row0:row0 + sub], j, row0, sub,
                   cw_ref[:, sl], cb_ref[:, sl], wg_ref[j], br_ref[:, sl], bi_ref[:, sl], c1[:, sl])

    def scan_piece(row0):
        carries[:] = [_scan_group(a_ref, b_ref, ho_ref, j, row0, row0, carries[j], rows, False)
                      for j in range(n_slab)]

    n_tok = f_ref.shape[1]
    kw, nw = n_tok // k_split, df // n_split

    def token_dft_piece(w_ref, part, n, k):
        cols = slice(part * df + n * nw, part * df + (n + 1) * nw)
        r = jnp.dot(w_ref[:, k * kw:(k + 1) * kw], f_ref[0, k * kw:(k + 1) * kw, n * nw:(n + 1) * nw],
                    preferred_element_type=F32)
        pq_ref[:, cols] = r if k == 0 else pq_ref[:, cols] + r

    def channel_dft_piece():
        fo_ref[0] = jnp.dot(pq_ref[...].astype(BF16), d2_ref[...],
                            preferred_element_type=F32).astype(fo_ref.dtype)

    blocks = list(range(0, tt, sub))
    vector_pieces, prev_scans = [], []
    for r in blocks:
        vector_pieces += _interleave([functools.partial(gates_piece, r, j) for j in range(n_slab)], prev_scans)
        prev_scans = [functools.partial(scan_piece, r + base) for base in range(0, sub, SCAN_ROWS)]
    matmul_pieces = [functools.partial(token_dft_piece, w_ref, part, n, k)
                     for part, w_ref in enumerate((wc_ref, ws_ref))
                     for n in range(n_split) for k in range(k_split)]
    for piece in _interleave(vector_pieces, matmul_pieces) + [channel_dft_piece] + prev_scans:
        piece()

    for j in range(n_slab):
        carry_ref[j] = carries[j]
        hf_ref[0, 0, :, j * LANES:(j + 1) * LANES] = ho_ref[j]


def _fwd(l, conv_w, conv_b, wg, br, bi, lam, h0, f, wc, ws, d2, tt=512, sub=128, n_split=2, k_split=4):
    bsz, t, ch = l.shape
    df = f.shape[2]
    nc = t // tt
    n_slab = ch // LANES
    return pl.pallas_call(
        functools.partial(_fwd_kernel, tt=tt, nc=nc, sub=sub, n_split=n_split, k_split=k_split),
        grid=(bsz, nc),
        in_specs=_halo_specs(tt, t, ch, lambda b, c: (b, c)) + [
            _resident(conv_w.shape), _resident((1, ch)), _resident(wg.shape),
            _resident((1, ch)), _resident((1, ch)), _resident((1, ch)),
            pl.BlockSpec((1, 1, ch), lambda b, c: (b, 0, 0)),
            pl.BlockSpec((1, t, df), lambda b, c: (b, 0, 0)),
            pl.BlockSpec((tt, t), lambda b, c: (c, 0)),
            pl.BlockSpec((tt, t), lambda b, c: (c, 0)),
            _resident(d2.shape)],
        out_specs=[pl.BlockSpec((1, 1, tt, ch), lambda b, c: (0, b, c, 0)),
                   pl.BlockSpec((1, tt, df), lambda b, c: (b, c, 0))],
        out_shape=[jax.ShapeDtypeStruct((1, bsz, t, ch), F32),
                   jax.ShapeDtypeStruct(f.shape, BF16)],
        scratch_shapes=[pltpu.VMEM((n_slab, tt + 2 * SUBLANES, LANES), F32),
                        pltpu.VMEM((n_slab, tt, LANES), F32),
                        pltpu.VMEM((n_slab, tt, LANES), F32),
                        pltpu.VMEM((n_slab, tt, LANES), F32),
                        pltpu.VMEM((n_slab, SUBLANES, LANES), F32),
                        pltpu.VMEM((tt, 2 * df), F32)],
        compiler_params=_params(("arbitrary", "arbitrary")),
        name="fwd_fourier",
    )(l, l, l, conv_w, conv_b, wg, br, bi, lam, h0, f, wc, ws, d2)


def _tail_kernel(l_ref, lp_ref, ln_ref, cw_ref, cb_ref, wg_ref, br_ref, bi_ref, lam_ref, h0_ref,
                 x_ref, hf_ref, g_ref, ga_ref, gb_ref, fo_ref,
                 gate2_ref, sh3_ref, sc3_ref, gate3_ref, gn_ref, gpre_ref, gpost_ref,
                 wfa_ref, wfb_ref, wo_ref, wi2_ref, wo2_ref,
                 o_ref, ext_ref, a_ref, b_ref, hb_ref, carry_ref, act_ref, *, tm, nt, n_tiles, half, chunk):
    s = pl.program_id(0)
    q = jnp.minimum(s, n_tiles - 1)
    cq = nt - 1 - q % nt
    n_slab = a_ref.shape[0]

    @pl.when(s == 0)
    def _():
        hb_ref[...] = jnp.zeros(hb_ref.shape, F32)
        carry_ref[...] = jnp.zeros(carry_ref.shape, F32)

    hb = jnp.concatenate([hb_ref[j] for j in range(n_slab)], axis=1)
    u = ((hf_ref[0, 0] + hb) * g_ref[0].astype(F32)).astype(BF16)
    yb = jnp.dot(u, wfb_ref[...], preferred_element_type=F32)
    ya = jnp.dot(fo_ref[0], wfa_ref[...], preferred_element_type=F32)
    m = ga_ref[0].astype(F32) * ya + gb_ref[0].astype(F32) * yb
    mx = jnp.dot(m.astype(BF16), wo_ref[...], preferred_element_type=F32)
    x2 = x_ref[0] + gate2_ref[0] * _rms(mx, gn_ref[...])

    c1 = _lru_c1(lam_ref[...])
    rows = lax.broadcasted_iota(jnp.int32, (SUBLANES, LANES), 0)
    carries = []
    for j in range(n_slab):
        sl = slice(j * LANES, (j + 1) * LANES)
        _lru_fill_ext(ext_ref, j, lp_ref[0, :, sl], l_ref[0, :, sl], ln_ref[0, :, sl], cq == 0, cq == nt - 1)
        h0 = jnp.broadcast_to(h0_ref[0, :, sl], (SUBLANES, LANES))
        carries.append(jnp.where(cq == nt - 1, h0, carry_ref[j]))

    def gates_piece(row0, j):
        sl = slice(j * LANES, (j + 1) * LANES)
        _lru_gates(ext_ref, a_ref, b_ref, j, row0, half, cw_ref[:, sl], cb_ref[:, sl], wg_ref[j],
                   br_ref[:, sl], bi_ref[:, sl], c1[:, sl])

    def scan_piece(row0, base):
        carries[:] = [_scan_group(a_ref, b_ref, hb_ref, j, base, row0 + base, carries[j], rows, True)
                      for j in range(n_slab)]

    pieces = []
    for row0 in range(tm - half, -1, -half):
        pieces += [functools.partial(gates_piece, row0, j) for j in range(n_slab)]
        pieces += [functools.partial(scan_piece, row0, base) for base in range(half - SCAN_ROWS, -1, -SCAN_ROWS)]

    def between(i, n):
        for piece in pieces[i * len(pieces) // n:(i + 1) * len(pieces) // n]:
            piece()

    o_ref[0] = _ffn_body(x2, sh3_ref[0], sc3_ref[0], gate3_ref[0], gpre_ref[...], gpost_ref[...],
                         wi2_ref, wo2_ref, act_ref, chunk, between)
    for j in range(n_slab):
        carry_ref[j] = carries[j]


def _tail(l, conv_w, conv_b, wg, br, bi, lam, h0, x, hf, g, ga, gb, fo, mod, gn, gpre, gpost,
          wfa, wfb, wo, wi2, wo2, tm=512, half=256, chunk=MXU_DIM):
    bsz, seq, d = x.shape
    ch = l.shape[2]
    nt = seq // tm
    n_tiles = bsz * nt
    n_slab = ch // LANES
    dff = wo2.shape[0]

    def scan_tile(s):
        q = jnp.minimum(s, n_tiles - 1)
        return q // nt, nt - 1 - q % nt

    def merge_tile(s):
        r = jnp.maximum(s - 1, 0)
        return r // nt, nt - 1 - r % nt

    tok = lambda w: pl.BlockSpec((1, tm, w), lambda s: merge_tile(s) + (0,))
    mod_spec = lambda k: pl.BlockSpec((1, 1, d), lambda s: (merge_tile(s)[0], 0, k))
    return pl.pallas_call(
        functools.partial(_tail_kernel, tm=tm, nt=nt, n_tiles=n_tiles, half=half, chunk=chunk),
        grid=(n_tiles + 1,),
        in_specs=_halo_specs(tm, seq, ch, scan_tile) + [
            _resident(conv_w.shape), _resident((1, ch)), _resident(wg.shape),
            _resident((1, ch)), _resident((1, ch)), _resident((1, ch)),
            pl.BlockSpec((1, 1, ch), lambda s: (scan_tile(s)[0], 0, 0)),
            tok(d),
            pl.BlockSpec((1, 1, tm, ch), lambda s: (0,) + merge_tile(s) + (0,)),
            tok(ch), tok(d), tok(d), tok(fo.shape[2]),
            mod_spec(5), mod_spec(6), mod_spec(7), mod_spec(8),
            _resident((1, d)), _resident((1, d)), _resident((1, d)),
            _resident(wfa.shape), _resident(wfb.shape), _resident(wo.shape),
            _resident(wi2.shape), _resident(wo2.shape)],
        out_specs=tok(d),
        out_shape=jax.ShapeDtypeStruct(x.shape, F32),
        scratch_shapes=[pltpu.VMEM((n_slab, tm + 2 * SUBLANES, LANES), F32),
                        pltpu.VMEM((n_slab, half, LANES), F32),
                        pltpu.VMEM((n_slab, half, LANES), F32),
                        pltpu.VMEM((n_slab, tm, LANES), F32),
                        pltpu.VMEM((n_slab, SUBLANES, LANES), F32),
                        pltpu.VMEM((tm, dff), BF16)],
        compiler_params=_params(("arbitrary",), TAIL_VMEM_LIMIT_BYTES),
        name="tail",
    )(l, l, l, conv_w, conv_b, wg, br, bi, lam, h0,
      x, hf, g, ga, gb, fo, mod, mod, mod, mod, gn, gpre, gpost, wfa, wfb, wo, wi2, wo2)


def kernel(x, c, ctx, c_ctx, w_ada, b_ada, norm_g, w_ffn1_in, w_ffn1_out, w_ffn2_in, w_ffn2_out,
           w_in, conv_w, conv_b, w_r, b_r, w_i, b_i, lam, w_fa, w_fb, w_out):
    depth = w_ada.shape[0]
    assert depth == 1, "single-layer problem: the context stream is only needed up to the mixer scans"
    bsz, seq, d = x.shape
    d_lru = w_fb.shape[1]
    d_f = w_fa.shape[1]
    lyr = 0

    pad = (-(bsz + 1)) % SUBLANES
    cc = jnp.concatenate([c, c_ctx[None, :], jnp.zeros((pad, d), F32)], axis=0)
    mod = _ada(cc, w_ada[lyr], b_ada[lyr][None, :])
    mod = mod.reshape(mod.shape[0], 1, N_MOD * d)
    lat_row = lambda b: b
    ctx_row = lambda b: bsz
    g = norm_g[lyr][:, None, :]

    wi1, wo1 = w_ffn1_in[lyr].astype(BF16), w_ffn1_out[lyr].astype(BF16)
    wi2, wo2 = w_ffn2_in[lyr].astype(BF16), w_ffn2_out[lyr].astype(BF16)
    w_in_b = w_in[lyr].astype(BF16)

    x1 = _ffn(x, mod, lat_row, 0, g[0], g[1], wi1, wo1)
    ctx_flat = ctx.reshape(1, bsz * ctx.shape[1], d)
    c1 = _ffn(ctx_flat, mod, ctx_row, 0, g[0], g[1], wi1, wo1)

    splits = ((0, d_f), (d_f, d_lru), (d_f + d_lru, d_lru), (d_f + 2 * d_lru, d), (d_f + 2 * d_lru + d, d))
    fx, lx, gx, gax, gbx = _inproj(x1, mod, lat_row, 3, g[2], w_in_b, splits, (BF16, F32, BF16, BF16, BF16),
                                   (None, None, _gelu_tanh, _sigmoid, _sigmoid))
    (lc,) = _inproj(c1, mod, ctx_row, 3, g[2], w_in_b[:, d_f:d_f + d_lru], ((0, d_lru),), (F32,), (None,))
    lc = lc.reshape(bsz, ctx.shape[1], d_lru)

    wgate = (0.5 * jnp.concatenate([w_r[lyr], w_i[lyr]], axis=-1)).astype(BF16)
    br, bi, lm = 0.5 * b_r[lyr][:, None, :], 0.5 * b_i[lyr][:, None, :], lam[lyr][:, None, :]
    cw, cb = conv_w[lyr], conv_b[lyr][None, :]
    h_ctx = _rglru(lc, cw, cb, wgate, br, bi, lm, jnp.zeros((2, bsz, 1, d_lru), F32), 2, tt=ctx.shape[1])
    h0f = h_ctx[0, :, -1][:, None, :]
    h0b = h_ctx[1, :, 0][:, None, :]

    wc, ws, d2 = _dft_tables(seq // GRID_W, GRID_W, d_f // FOURIER_GROUPS, FOURIER_GROUPS)
    hf, fo = _fwd(lx, cw, cb, wgate[0], br[0], bi[0], lm[0], h0f, fx, wc, ws, d2)

    return _tail(lx, cw, cb, wgate[1], br[1], bi[1], lm[1], h0b, x1, hf, gx, gax, gbx, fo, mod,
                 g[3], g[4], g[5], w_fa[lyr].astype(BF16), w_fb[lyr].astype(BF16), w_out[lyr].astype(BF16),
                 wi2, wo2)
```

```python
import functools
import math

import numpy as np
import jax
import jax.numpy as jnp
from jax import lax
from jax.experimental import pallas as pl
from jax.experimental.pallas import tpu as pltpu

LANES = 128
SUBLANES = 8
MXU_DIM = 256
VMEM_LIMIT_BYTES = 56 * 1024 * 1024
TAIL_VMEM_LIMIT_BYTES = 62 * 1024 * 1024

GRID_W = 64
FOURIER_GROUPS = 4
CONV_W = 4
CONV_LEFT = (CONV_W - 1) // 2
GATE_C = 8.0
N_MOD = 9
EPS = 1e-6

BF16 = jnp.bfloat16
F32 = jnp.float32

SEG_LEN = 4
SCAN_ROWS = SEG_LEN * SUBLANES


def _sigmoid(x):
    return 0.5 * jnp.tanh(0.5 * x) + 0.5


def _silu(x):
    return x * _sigmoid(x)


def _gelu_tanh(x):
    c = math.sqrt(2.0 / math.pi)
    return 0.5 * x * (1.0 + jnp.tanh(c * (x + 0.044715 * (x * x * x))))


def _rms(x, g):
    return x * lax.rsqrt(jnp.mean(x * x, axis=-1, keepdims=True) + EPS) * g


def _resident(shape):
    nd = len(shape)
    return pl.BlockSpec(shape, lambda *_: (0,) * nd, pipeline_mode=pl.Buffered(1))


def _params(semantics, vmem=VMEM_LIMIT_BYTES):
    return pltpu.CompilerParams(dimension_semantics=semantics, vmem_limit_bytes=vmem)


def _interleave(major, minor):
    merged, done = [], 0
    for i, piece in enumerate(major):
        while done < len(minor) and done * len(major) <= i * len(minor):
            merged.append(minor[done])
            done += 1
        merged.append(piece)
    return merged + list(minor[done:])


def _ada_kernel(c_ref, w_ref, b_ref, o_ref):
    s = _silu(c_ref[...]).astype(BF16)
    o_ref[...] = jnp.dot(s, w_ref[...].astype(BF16), preferred_element_type=F32) + b_ref[...]


def _ada(cc, w, b, tn=1536):
    m, d = cc.shape
    n = w.shape[1]
    return pl.pallas_call(
        _ada_kernel,
        grid=(n // tn,),
        in_specs=[pl.BlockSpec((m, d), lambda j: (0, 0)),
                  pl.BlockSpec((d, tn), lambda j: (0, j)),
                  pl.BlockSpec((1, tn), lambda j: (0, j))],
        out_specs=pl.BlockSpec((m, tn), lambda j: (0, j)),
        out_shape=jax.ShapeDtypeStruct((m, n), F32),
        compiler_params=_params(("arbitrary",)),
        name="adaln",
    )(cc, w, b)


def _ffn_pre(x, sh, sc, gpre):
    return (_rms(x, gpre) * (1.0 + sc) + sh).astype(BF16)


def _ffn_hidden(h, wi_ref, act_ref, chunk, between=None):
    dff = act_ref.shape[1]
    for i, c in enumerate(range(0, dff, chunk)):
        gate = jnp.dot(h[...], wi_ref[:, c:c + chunk], preferred_element_type=F32)
        up = jnp.dot(h[...], wi_ref[:, dff + c:dff + c + chunk], preferred_element_type=F32)
        act_ref[:, c:c + chunk] = (_silu(gate) * up).astype(BF16)
        if between is not None:
            between(i, dff // chunk)


def _ffn_post(x, ga, gpost, act_ref, wo_ref):
    y = jnp.dot(act_ref[...], wo_ref[...], preferred_element_type=F32)
    return x + (0.5 * ga) * _rms(y, gpost)


def _ffn_body(x, sh, sc, ga, gpre, gpost, wi_ref, wo_ref, act_ref, chunk):
    _ffn_hidden(_ffn_pre(x, sh, sc, gpre), wi_ref, act_ref, chunk)
    return _ffn_post(x, ga, gpost, act_ref, wo_ref)


def _ffn_kernel(x_ref, sh_ref, sc_ref, ga_ref, gpre_ref, gpost_ref, wi_ref, wo_ref, *rest, chunk, n_cast):
    cast_in, o_ref, cast_out, act_ref = rest[:n_cast], rest[n_cast], rest[n_cast + 1:-1], rest[-1]
    o_ref[0] = _ffn_body(x_ref[0], sh_ref[0], sc_ref[0], ga_ref[0], gpre_ref[...], gpost_ref[...],
                         wi_ref, wo_ref, act_ref, chunk)
    for src, dst in zip(cast_in, cast_out):
        dst[...] = src[...].astype(dst.dtype)


def _ffn(x, mod, mod_row, k_shift, g_pre, g_post, wi, wo, casts=(), tm=512, chunk=MXU_DIM):
    bsz, seq, d = x.shape
    dff = wo.shape[0]
    nt = seq // tm
    n_steps = bsz * nt
    mod_spec = lambda k: pl.BlockSpec((1, 1, d), lambda b, i: (mod_row(b), 0, k))

    def cast_spec(w):
        rows = w.shape[0]
        n_blocks = max(n for n in range(1, n_steps + 1)
                       if n_steps % n == 0 and rows % n == 0 and (rows // n) % (2 * SUBLANES) == 0)
        rep = n_steps // n_blocks
        return pl.BlockSpec((rows // n_blocks, w.shape[1]), lambda b, i: ((b * nt + i) // rep, 0))

    cast_specs = lambda: [cast_spec(w) for w in casts]
    out = pl.pallas_call(
        functools.partial(_ffn_kernel, chunk=chunk, n_cast=len(casts)),
        grid=(bsz, nt),
        in_specs=[pl.BlockSpec((1, tm, d), lambda b, i: (b, i, 0)),
                  mod_spec(k_shift), mod_spec(k_shift + 1), mod_spec(k_shift + 2),
                  _resident((1, d)), _resident((1, d)),
                  _resident(wi.shape), _resident(wo.shape)] + cast_specs(),
        out_specs=[pl.BlockSpec((1, tm, d), lambda b, i: (b, i, 0))] + cast_specs(),
        out_shape=[jax.ShapeDtypeStruct(x.shape, F32)] + [jax.ShapeDtypeStruct(w.shape, BF16) for w in casts],
        scratch_shapes=[pltpu.VMEM((tm, dff), BF16)],
        compiler_params=_params(("arbitrary", "arbitrary")),
        name="ffn",
    )(x, mod, mod, mod, g_pre, g_post, wi, wo, *casts)
    return out if casts else out[0]


def _inproj_kernel(x_ref, sh_ref, sc_ref, g_ref, w_ref, *o_refs, splits, acts, chunk):
    h = (_rms(x_ref[0], g_ref[...]) * (1.0 + sc_ref[0]) + sh_ref[0]).astype(BF16)
    for o_ref, (start, width), act in zip(o_refs, splits, acts):
        for c in range(0, width, chunk):
            r = jnp.dot(h, w_ref[:, start + c:start + c + chunk], preferred_element_type=F32)
            if act is not None:
                r = act(r)
            o_ref[0, :, c:c + chunk] = r.astype(o_ref.dtype)


def _inproj(x, mod, mod_row, k_shift, g, w, splits, dtypes, acts, tm=512, chunk=512):
    bsz, seq, d = x.shape
    mod_spec = lambda k: pl.BlockSpec((1, 1, d), lambda b, i: (mod_row(b), 0, k))
    return pl.pallas_call(
        functools.partial(_inproj_kernel, splits=splits, acts=acts, chunk=chunk),
        grid=(bsz, seq // tm),
        in_specs=[pl.BlockSpec((1, tm, d), lambda b, i: (b, i, 0)),
                  mod_spec(k_shift), mod_spec(k_shift + 1),
                  _resident((1, d)), _resident(w.shape)],
        out_specs=[pl.BlockSpec((1, tm, wd), lambda b, i: (b, i, 0)) for _, wd in splits],
        out_shape=[jax.ShapeDtypeStruct((bsz, seq, wd), dt) for (_, wd), dt in zip(splits, dtypes)],
        compiler_params=_params(("arbitrary", "arbitrary")),
        name="inproj",
    )(x, mod, mod, g, w)


def _lru_c1(lam):
    e = jnp.exp(-jnp.abs(lam))
    u = 1.0 + e
    log1p_e = jnp.where(u == 1.0, e, jnp.log(u) * (e / (u - 1.0)))
    return (-0.5 * GATE_C * math.log2(math.e)) * (jnp.maximum(-lam, 0.0) + log1p_e)


def _lru_fill_ext(ext_ref, j, prev, main, nxt, at_start, at_end):
    tt = main.shape[0]
    ext_ref[j, 0:SUBLANES, :] = jnp.where(at_start, 0.0, prev)
    ext_ref[j, SUBLANES:SUBLANES + tt, :] = main
    ext_ref[j, SUBLANES + tt:2 * SUBLANES + tt, :] = jnp.where(at_end, 0.0, nxt)


def _lru_gates(ext_ref, a_ref, b_ref, j, row0, nrows, cw, cb, wg, br, bi, c1):
    _lru_gate_math(a_ref, b_ref, j, _lru_conv(ext_ref, j, row0, nrows, cw, cb), wg, br, bi, c1)


def _lru_conv(ext_ref, j, row0, nrows, cw, cb):
    xc = cb
    for k in range(CONV_W):
        off = SUBLANES - CONV_LEFT + k + row0
        tap = jnp.concatenate(
            [ext_ref[j, pl.ds(off + m + kk, SUBLANES, stride=SEG_LEN), :]
             for m in range(0, nrows, SCAN_ROWS) for kk in range(SEG_LEN)], axis=0)
        xc = xc + tap * cw[k:k + 1, :]
    return xc


def _lru_gate_math(a_ref, b_ref, j, xc, wg, br, bi, c1):
    z = jnp.dot(xc.astype(BF16), wg, preferred_element_type=F32)
    t_r = jnp.tanh(z[:, :LANES] + br)
    t_i = jnp.tanh(z[:, LANES:] + bi)
    a = jnp.exp2(c1 * t_r + c1)
    y = 1.0 - a * a
    root = jnp.where(y > 0.0, y * lax.rsqrt(y), 0.0)
    hx = 0.5 * xc
    a_ref[j] = a
    b_ref[j] = root * (t_i * hx + hx)


def _shift_rows(x, k, fill, rows):
    return jnp.where(rows >= k, pltpu.roll(x, k, axis=0), fill)


def _shift_rows_up(x, k, fill, rows):
    return jnp.where(rows < SUBLANES - k, pltpu.roll(x, SUBLANES - k, axis=0), fill)


def _scan_group(a_ref, b_ref, o_ref, j, base, out_base, carry, rows, reverse):
    ks = range(SEG_LEN - 1, -1, -1) if reverse else range(SEG_LEN)
    shift = _shift_rows_up if reverse else _shift_rows
    a = {k: a_ref[j, pl.ds(base + k * SUBLANES, SUBLANES), :] for k in ks}
    b = {k: b_ref[j, pl.ds(base + k * SUBLANES, SUBLANES), :] for k in ks}
    h, p = {}, {}
    prev = None
    for k in ks:
        if prev is None:
            h[k], p[k] = b[k], a[k]
        else:
            h[k] = a[k] * h[prev] + b[k]
            p[k] = a[k] * p[prev]
        prev = k
    hc, pc = h[prev], p[prev]
    for dd in (1, 2, 4):
        hc = pc * shift(hc, dd, 0.0, rows) + hc
        pc = pc * shift(pc, dd, 1.0, rows)
    g = hc + pc * carry
    cin = shift(g, 1, carry, rows)
    for k in ks:
        o_ref[j, pl.ds(out_base + k, SUBLANES, stride=SEG_LEN), :] = h[k] + p[k] * cin
    last = 0 if reverse else SUBLANES - 1
    return jnp.broadcast_to(g[last:last + 1, :], (SUBLANES, LANES))


def _rec_kernel(l_ref, lp_ref, ln_ref, cw_ref, cb_ref, wg_ref, br_ref, bi_ref, lam_ref, h0_ref,
                o_ref, ext_ref, a_ref, b_ref, ho_ref, carry_ref, *, tt, nc):
    d = pl.program_id(0)
    c = pl.program_id(2)
    tc = c + d * (nc - 1 - 2 * c)
    n_slab = a_ref.shape[0]
    c1 = _lru_c1(lam_ref[0])

    for j in range(n_slab):
        sl = slice(j * LANES, (j + 1) * LANES)
        _lru_fill_ext(ext_ref, j, lp_ref[0, :, sl], l_ref[0, :, sl], ln_ref[0, :, sl], tc == 0, tc == nc - 1)
        _lru_gates(ext_ref, a_ref, b_ref, j, 0, tt, cw_ref[:, sl], cb_ref[:, sl], wg_ref[0, j],
                   br_ref[0, :, sl], bi_ref[0, :, sl], c1[:, sl])

    @pl.when(c == 0)
    def _():
        for j in range(n_slab):
            carry_ref[j] = jnp.broadcast_to(h0_ref[0, 0, :, j * LANES:(j + 1) * LANES], (SUBLANES, LANES))

    rows = lax.broadcasted_iota(jnp.int32, (SUBLANES, LANES), 0)
    n_groups = tt // SCAN_ROWS

    def run(reverse):
        def body(m, carries):
            base = pl.multiple_of(((n_groups - 1 - m) if reverse else m) * SCAN_ROWS, SCAN_ROWS)
            return tuple(_scan_group(a_ref, b_ref, ho_ref, j, base, base, carries[j], rows, reverse)
                         for j in range(n_slab))
        carries = lax.fori_loop(0, n_groups, body, tuple(carry_ref[j] for j in range(n_slab)))
        for j in range(n_slab):
            carry_ref[j] = carries[j]

    @pl.when(d == 0)
    def _():
        run(False)

    @pl.when(d == 1)
    def _():
        run(True)

    for j in range(n_slab):
        o_ref[0, 0, :, j * LANES:(j + 1) * LANES] = ho_ref[j].astype(o_ref.dtype)


def _halo_specs(tt, t, ch, idx):
    hb = tt // SUBLANES
    last = t // SUBLANES - 1

    def prev_map(*g):
        b, tc = idx(*g)
        return b, jnp.maximum(tc * hb - 1, 0), 0

    def next_map(*g):
        b, tc = idx(*g)
        return b, jnp.minimum((tc + 1) * hb, last), 0

    return [pl.BlockSpec((1, tt, ch), lambda *g: idx(*g) + (0,)),
            pl.BlockSpec((1, SUBLANES, ch), prev_map),
            pl.BlockSpec((1, SUBLANES, ch), next_map)]


def _rglru(l, conv_w, conv_b, wg, br, bi, lam, h0, n_dir, tt=256):
    bsz, t, ch = l.shape
    nc = t // tt
    n_slab = ch // LANES
    tcf = lambda d, c: c + d * (nc - 1 - 2 * c)
    vec = lambda: pl.BlockSpec((1, 1, ch), lambda d, b, c: (d, 0, 0))
    return pl.pallas_call(
        functools.partial(_rec_kernel, tt=tt, nc=nc),
        grid=(n_dir, bsz, nc),
        in_specs=_halo_specs(tt, t, ch, lambda d, b, c: (b, tcf(d, c))) + [
            pl.BlockSpec(conv_w.shape, lambda d, b, c: (0, 0)),
            pl.BlockSpec((1, ch), lambda d, b, c: (0, 0)),
            pl.BlockSpec((1,) + wg.shape[1:], lambda d, b, c: (d, 0, 0, 0)),
            vec(), vec(), vec(),
            pl.BlockSpec((1, 1, 1, ch), lambda d, b, c: (d, b, 0, 0))],
        out_specs=pl.BlockSpec((1, 1, tt, ch), lambda d, b, c: (d, b, tcf(d, c), 0)),
        out_shape=jax.ShapeDtypeStruct((n_dir, bsz, t, ch), F32),
        scratch_shapes=[pltpu.VMEM((n_slab, tt + 2 * SUBLANES, LANES), F32),
                        pltpu.VMEM((n_slab, tt, LANES), F32),
                        pltpu.VMEM((n_slab, tt, LANES), F32),
                        pltpu.VMEM((n_slab, tt, LANES), F32),
                        pltpu.VMEM((n_slab, SUBLANES, LANES), F32)],
        compiler_params=_params(("arbitrary", "arbitrary", "arbitrary")),
        name="rglru",
    )(l, l, l, conv_w, conv_b, wg, br, bi, lam, h0)


def _dft_tables(rows, cols, group_w, groups):
    n_tok = rows * cols
    scale = 1.0 / math.sqrt(n_tok * group_w)
    ch = np.arange(group_w)
    ph = 2.0 * np.pi * ((ch[:, None] * ch[None, :]) % group_w) / group_w
    d_f = groups * group_w
    d2 = np.zeros((2 * d_f, d_f), np.float64)
    for gi in range(groups):
        s = slice(gi * group_w, (gi + 1) * group_w)
        d2[s, s] = np.cos(ph)
        d2[d_f + gi * group_w:d_f + (gi + 1) * group_w, s] = np.sin(ph)
    t = np.arange(n_tok)
    r, c = t // cols, t % cols
    th = 2.0 * np.pi * (((r[:, None] * r[None, :]) % rows) / rows + ((c[:, None] * c[None, :]) % cols) / cols)
    wc = np.cos(th) * scale
    ws = -np.sin(th) * scale
    as_bf16 = lambda m: jnp.asarray(m.astype(np.float32)).astype(BF16)
    return as_bf16(wc), as_bf16(ws), as_bf16(d2)


def _fwd_kernel(l_ref, lp_ref, ln_ref, cw_ref, cb_ref, wg_ref, br_ref, bi_ref, lam_ref, h0_ref,
                f_ref, wc_ref, ws_ref, d2_ref,
                hf_ref, fo_ref, ext_ref, a_ref, b_ref, ho_ref, carry_ref, pq_ref,
                *, tt, nc, sub, n_split, k_split):
    c = pl.program_id(1)
    n_slab = a_ref.shape[0]
    df = f_ref.shape[2]

    @pl.when((pl.program_id(0) == 0) & (c == 0))
    def _():
        carry_ref[...] = jnp.zeros(carry_ref.shape, F32)

    c1 = _lru_c1(lam_ref[...])
    rows = lax.broadcasted_iota(jnp.int32, (SUBLANES, LANES), 0)
    carries = []
    for j in range(n_slab):
        sl = slice(j * LANES, (j + 1) * LANES)
        _lru_fill_ext(ext_ref, j, lp_ref[0, :, sl], l_ref[0, :, sl], ln_ref[0, :, sl], c == 0, c == nc - 1)
        h0 = jnp.broadcast_to(h0_ref[0, :, sl], (SUBLANES, LANES))
        carries.append(jnp.where(c == 0, h0, carry_ref[j]))

    def gates_piece(row0, j):
        sl = slice(j * LANES, (j + 1) * LANES)
        _lru_gates(ext_ref, a_ref.at[:, row0:row0 + sub], b_ref.at[:, row0:row0 + sub], j, row0, sub,
                   cw_ref[:, sl], cb_ref[:, sl], wg_ref[j], br_ref[:, sl], bi_ref[:, sl], c1[:, sl])

    def scan_piece(row0):
        carries[:] = [_scan_group(a_ref, b_ref, ho_ref, j, row0, row0, carries[j], rows, False)
                      for j in range(n_slab)]

    n_tok = f_ref.shape[1]
    kw, nw = n_tok // k_split, df // n_split

    def token_dft_piece(w_ref, part, n, k):
        cols = slice(part * df + n * nw, part * df + (n + 1) * nw)
        r = jnp.dot(w_ref[:, k * kw:(k + 1) * kw], f_ref[0, k * kw:(k + 1) * kw, n * nw:(n + 1) * nw],
                    preferred_element_type=F32)
        pq_ref[:, cols] = r if k == 0 else pq_ref[:, cols] + r

    def channel_dft_piece():
        fo_ref[0] = jnp.dot(pq_ref[...].astype(BF16), d2_ref[...],
                            preferred_element_type=F32).astype(fo_ref.dtype)

    blocks = list(range(0, tt, sub))
    vector_pieces, prev_scans = [], []
    for r in blocks:
        vector_pieces += _interleave([functools.partial(gates_piece, r, j) for j in range(n_slab)], prev_scans)
        prev_scans = [functools.partial(scan_piece, r + base) for base in range(0, sub, SCAN_ROWS)]
    matmul_pieces = [functools.partial(token_dft_piece, w_ref, part, n, k)
                     for part, w_ref in enumerate((wc_ref, ws_ref))
                     for n in range(n_split) for k in range(k_split)]
    for piece in _interleave(vector_pieces, matmul_pieces) + [channel_dft_piece] + prev_scans:
        piece()

    for j in range(n_slab):
        carry_ref[j] = carries[j]
        hf_ref[0, 0, :, j * LANES:(j + 1) * LANES] = ho_ref[j].astype(hf_ref.dtype)


def _fwd(l, conv_w, conv_b, wg, br, bi, lam, h0, f, wc, ws, d2, tt=512, sub=128, n_split=2, k_split=4):
    bsz, t, ch = l.shape
    df = f.shape[2]
    nc = t // tt
    n_slab = ch // LANES
    return pl.pallas_call(
        functools.partial(_fwd_kernel, tt=tt, nc=nc, sub=sub, n_split=n_split, k_split=k_split),
        grid=(bsz, nc),
        in_specs=_halo_specs(tt, t, ch, lambda b, c: (b, c)) + [
            _resident(conv_w.shape), _resident((1, ch)), _resident(wg.shape),
            _resident((1, ch)), _resident((1, ch)), _resident((1, ch)),
            pl.BlockSpec((1, 1, ch), lambda b, c: (b, 0, 0)),
            pl.BlockSpec((1, t, df), lambda b, c: (b, 0, 0)),
            pl.BlockSpec((tt, t), lambda b, c: (c, 0)),
            pl.BlockSpec((tt, t), lambda b, c: (c, 0)),
            _resident(d2.shape)],
        out_specs=[pl.BlockSpec((1, 1, tt, ch), lambda b, c: (0, b, c, 0)),
                   pl.BlockSpec((1, tt, df), lambda b, c: (b, c, 0))],
        out_shape=[jax.ShapeDtypeStruct((1, bsz, t, ch), BF16),
                   jax.ShapeDtypeStruct(f.shape, BF16)],
        scratch_shapes=[pltpu.VMEM((n_slab, tt + 2 * SUBLANES, LANES), F32),
                        pltpu.VMEM((n_slab, tt, LANES), F32),
                        pltpu.VMEM((n_slab, tt, LANES), F32),
                        pltpu.VMEM((n_slab, tt, LANES), F32),
                        pltpu.VMEM((n_slab, SUBLANES, LANES), F32),
                        pltpu.VMEM((tt, 2 * df), F32)],
        compiler_params=_params(("arbitrary", "arbitrary")),
        name="fwd_fourier",
    )(l, l, l, conv_w, conv_b, wg, br, bi, lam, h0, f, wc, ws, d2)


def _tail_kernel(l_ref, lp_ref, ln_ref, cw_ref, cb_ref, wg_ref, br_ref, bi_ref, lam_ref, h0_ref,
                 x_ref, hf_ref, g_ref, ga_ref, gb_ref, fo_ref,
                 gate2_ref, sh3_ref, sc3_ref, gate3_ref, gn_ref, gpre_ref, gpost_ref,
                 wfa_ref, wfb_ref, wo_ref, wi2_ref, wo2_ref,
                 o_ref, ext_ref, a_ref, b_ref, hb_ref, carry_ref, act_ref, h2_ref, x2_ref,
                 *, tm, nt, n_tiles, half, chunk):
    s = pl.program_id(0)
    q = jnp.minimum(s, n_tiles - 1)
    cq = nt - 1 - q % nt
    slot = lax.rem(s, 2)
    n_slab = a_ref.shape[0]

    @pl.when(s == 0)
    def _():
        hb_ref[...] = jnp.zeros(hb_ref.shape, F32)
        carry_ref[...] = jnp.zeros(carry_ref.shape, F32)
        h2_ref[...] = jnp.zeros(h2_ref.shape, BF16)
        x2_ref[...] = jnp.zeros(x2_ref.shape, F32)

    hb = jnp.concatenate([hb_ref[j] for j in range(n_slab)], axis=1)
    merged = {"u": ((hf_ref[0, 0].astype(F32) + hb) * g_ref[0].astype(F32)).astype(BF16)}

    def merge_branches():
        merged["yb"] = jnp.dot(merged.pop("u"), wfb_ref[...], preferred_element_type=F32)
        merged["ya"] = jnp.dot(fo_ref[0], wfa_ref[...], preferred_element_type=F32)

    def merge_gate():
        m = ga_ref[0].astype(F32) * merged.pop("ya") + gb_ref[0].astype(F32) * merged.pop("yb")
        merged["m"] = m.astype(BF16)

    def merge_project():
        merged["mx"] = jnp.dot(merged.pop("m"), wo_ref[...], preferred_element_type=F32)

    def merge_residual():
        x2 = x_ref[0] + gate2_ref[0] * _rms(merged.pop("mx"), gn_ref[...])
        x2_ref[slot] = x2
        merged["h"] = _ffn_pre(x2, sh3_ref[0], sc3_ref[0], gpre_ref[...])

    c1 = _lru_c1(lam_ref[...])
    rows = lax.broadcasted_iota(jnp.int32, (SUBLANES, LANES), 0)
    carries = []
    for j in range(n_slab):
        sl = slice(j * LANES, (j + 1) * LANES)
        _lru_fill_ext(ext_ref, j, lp_ref[0, :, sl], l_ref[0, :, sl], ln_ref[0, :, sl], cq == 0, cq == nt - 1)
        h0 = jnp.broadcast_to(h0_ref[0, :, sl], (SUBLANES, LANES))
        carries.append(jnp.where(cq == nt - 1, h0, carry_ref[j]))

    xcs = {}

    def conv_piece(row0, j):
        sl = slice(j * LANES, (j + 1) * LANES)
        xcs[row0, j] = _lru_conv(ext_ref, j, row0, half, cw_ref[:, sl], cb_ref[:, sl])

    def gate_piece(row0, j):
        sl = slice(j * LANES, (j + 1) * LANES)
        _lru_gate_math(a_ref, b_ref, j, xcs.pop((row0, j)), wg_ref[j], br_ref[:, sl], bi_ref[:, sl], c1[:, sl])

    def scan_piece(row0, base):
        carries[:] = [_scan_group(a_ref, b_ref, hb_ref, j, base, row0 + base, carries[j], rows, True)
                      for j in range(n_slab)]

    pieces = []
    for row0 in range(tm - half, -1, -half):
        pieces.append(functools.partial(conv_piece, row0, 0))
        for j in range(n_slab):
            if j + 1 < n_slab:
                pieces.append(functools.partial(conv_piece, row0, j + 1))
            pieces.append(functools.partial(gate_piece, row0, j))
        pieces += [functools.partial(scan_piece, row0, base) for base in range(half - SCAN_ROWS, -1, -SCAN_ROWS)]

    n_chunks = act_ref.shape[1] // chunk
    merge_at = {1: merge_branches, 3: merge_gate, 5: merge_project, 8: merge_residual}
    assert max(merge_at) < n_chunks

    def between(i, n):
        for piece in pieces[i * len(pieces) // n:(i + 1) * len(pieces) // n]:
            piece()
        if i in merge_at:
            merge_at[i]()

    _ffn_hidden(h2_ref, wi2_ref, act_ref, chunk, between)
    h2_ref[...] = merged.pop("h")
    o_ref[0] = _ffn_post(x2_ref[1 - slot], gate3_ref[0], gpost_ref[...], act_ref, wo2_ref)
    for j in range(n_slab):
        carry_ref[j] = carries[j]


def _tail(l, conv_w, conv_b, wg, br, bi, lam, h0, x, hf, g, ga, gb, fo, mod, gn, gpre, gpost,
          wfa, wfb, wo, wi2, wo2, tm=512, half=256, chunk=MXU_DIM):
    bsz, seq, d = x.shape
    ch = l.shape[2]
    nt = seq // tm
    n_tiles = bsz * nt
    n_slab = ch // LANES
    dff = wo2.shape[0]

    def scan_tile(s):
        q = jnp.minimum(s, n_tiles - 1)
        return q // nt, nt - 1 - q % nt

    def merge_tile(s):
        r = jnp.clip(s - 1, 0, n_tiles - 1)
        return r // nt, nt - 1 - r % nt

    def ffn_tile(s):
        r = jnp.clip(s - 2, 0, n_tiles - 1)
        return r // nt, nt - 1 - r % nt

    tok = lambda w: pl.BlockSpec((1, tm, w), lambda s: merge_tile(s) + (0,))
    mod_spec = lambda k, tile=merge_tile: pl.BlockSpec((1, 1, d), lambda s: (tile(s)[0], 0, k))
    return pl.pallas_call(
        functools.partial(_tail_kernel, tm=tm, nt=nt, n_tiles=n_tiles, half=half, chunk=chunk),
        grid=(n_tiles + 2,),
        in_specs=_halo_specs(tm, seq, ch, scan_tile) + [
            _resident(conv_w.shape), _resident((1, ch)), _resident(wg.shape),
            _resident((1, ch)), _resident((1, ch)), _resident((1, ch)),
            pl.BlockSpec((1, 1, ch), lambda s: (scan_tile(s)[0], 0, 0)),
            tok(d),
            pl.BlockSpec((1, 1, tm, ch), lambda s: (0,) + merge_tile(s) + (0,)),
            tok(ch), tok(d), tok(d), tok(fo.shape[2]),
            mod_spec(5), mod_spec(6), mod_spec(7), mod_spec(8, ffn_tile),
            _resident((1, d)), _resident((1, d)), _resident((1, d)),
            _resident(wfa.shape), _resident(wfb.shape), _resident(wo.shape),
            _resident(wi2.shape), _resident(wo2.shape)],
        out_specs=pl.BlockSpec((1, tm, d), lambda s: ffn_tile(s) + (0,)),
        out_shape=jax.ShapeDtypeStruct(x.shape, F32),
        scratch_shapes=[pltpu.VMEM((n_slab, tm + 2 * SUBLANES, LANES), F32),
                        pltpu.VMEM((n_slab, half, LANES), F32),
                        pltpu.VMEM((n_slab, half, LANES), F32),
                        pltpu.VMEM((n_slab, tm, LANES), F32),
                        pltpu.VMEM((n_slab, SUBLANES, LANES), F32),
                        pltpu.VMEM((tm, dff), BF16),
                        pltpu.VMEM((tm, d), BF16),
                        pltpu.VMEM((2, tm, d), F32)],
        compiler_params=_params(("arbitrary",), TAIL_VMEM_LIMIT_BYTES),
        name="tail",
    )(l, l, l, conv_w, conv_b, wg, br, bi, lam, h0,
      x, hf, g, ga, gb, fo, mod, mod, mod, mod, gn, gpre, gpost, wfa, wfb, wo, wi2, wo2)


def kernel(x, c, ctx, c_ctx, w_ada, b_ada, norm_g, w_ffn1_in, w_ffn1_out, w_ffn2_in, w_ffn2_out,
           w_in, conv_w, conv_b, w_r, b_r, w_i, b_i, lam, w_fa, w_fb, w_out):
    depth = w_ada.shape[0]
    assert depth == 1, "single-layer problem: the context stream is only needed up to the mixer scans"
    bsz, seq, d = x.shape
    d_lru = w_fb.shape[1]
    d_f = w_fa.shape[1]
    lyr = 0

    pad = (-(bsz + 1)) % SUBLANES
    cc = jnp.concatenate([c, c_ctx[None, :], jnp.zeros((pad, d), F32)], axis=0)
    mod = _ada(cc, w_ada[lyr], b_ada[lyr][None, :])
    mod = mod.reshape(mod.shape[0], 1, N_MOD * d)
    lat_row = lambda b: b
    ctx_row = lambda b: bsz
    g = norm_g[lyr][:, None, :]

    wi1, wo1 = w_ffn1_in[lyr].astype(BF16), w_ffn1_out[lyr].astype(BF16)

    x1, wi2, wo2, w_in_b, wfa_b, wfb_b, wout_b = _ffn(
        x, mod, lat_row, 0, g[0], g[1], wi1, wo1,
        casts=(w_ffn2_in[lyr], w_ffn2_out[lyr], w_in[lyr], w_fa[lyr], w_fb[lyr], w_out[lyr]))
    ctx_flat = ctx.reshape(1, bsz * ctx.shape[1], d)
    c1 = _ffn(ctx_flat, mod, ctx_row, 0, g[0], g[1], wi1, wo1)

    splits = ((0, d_f), (d_f, d_lru), (d_f + d_lru, d_lru), (d_f + 2 * d_lru, d), (d_f + 2 * d_lru + d, d))
    fx, lx, gx, gax, gbx = _inproj(x1, mod, lat_row, 3, g[2], w_in_b, splits, (BF16, F32, BF16, BF16, BF16),
                                   (None, None, _gelu_tanh, _sigmoid, _sigmoid))
    (lc,) = _inproj(c1, mod, ctx_row, 3, g[2], w_in_b[:, d_f:d_f + d_lru], ((0, d_lru),), (F32,), (None,))
    lc = lc.reshape(bsz, ctx.shape[1], d_lru)

    wgate = (0.5 * jnp.concatenate([w_r[lyr], w_i[lyr]], axis=-1)).astype(BF16)
    br, bi, lm = 0.5 * b_r[lyr][:, None, :], 0.5 * b_i[lyr][:, None, :], lam[lyr][:, None, :]
    cw, cb = conv_w[lyr], conv_b[lyr][None, :]
    h_ctx = _rglru(lc, cw, cb, wgate, br, bi, lm, jnp.zeros((2, bsz, 1, d_lru), F32), 2, tt=ctx.shape[1])
    h0f = h_ctx[0, :, -1][:, None, :]
    h0b = h_ctx[1, :, 0][:, None, :]

    wc, ws, d2 = _dft_tables(seq // GRID_W, GRID_W, d_f // FOURIER_GROUPS, FOURIER_GROUPS)
    hf, fo = _fwd(lx, cw, cb, wgate[0], br[0], bi[0], lm[0], h0f, fx, wc, ws, d2)

    return _tail(lx, cw, cb, wgate[1], br[1], bi[1], lm[1], h0b, x1, hf, gx, gax, gbx, fo, mod,
                 g[3], g[4], g[5], wfa_b, wfb_b, wout_b, wi2, wo2)
```

```python
import functools
import math

import numpy as np
import jax
import jax.numpy as jnp
from jax import lax
from jax.experimental import pallas as pl
from jax.experimental.pallas import tpu as pltpu

LANES = 128
SUBLANES = 8
MXU_DIM = 256
VMEM_LIMIT_BYTES = 56 * 1024 * 1024
TAIL_VMEM_LIMIT_BYTES = 62 * 1024 * 1024

GRID_W = 64
FOURIER_GROUPS = 4
CONV_W = 4
CONV_LEFT = (CONV_W - 1) // 2
GATE_C = 8.0
N_MOD = 9
EPS = 1e-6

BF16 = jnp.bfloat16
F32 = jnp.float32

SEG_LEN = 4
SCAN_ROWS = SEG_LEN * SUBLANES


def _sigmoid(x):
    return 0.5 * jnp.tanh(0.5 * x) + 0.5


def _silu(x):
    return x * _sigmoid(x)


def _gelu_tanh(x):
    c = math.sqrt(2.0 / math.pi)
    return 0.5 * x * (1.0 + jnp.tanh(c * (x + 0.044715 * (x * x * x))))


def _rms(x, g):
    return x * lax.rsqrt(jnp.mean(x * x, axis=-1, keepdims=True) + EPS) * g


def _resident(shape):
    nd = len(shape)
    return pl.BlockSpec(shape, lambda *_: (0,) * nd, pipeline_mode=pl.Buffered(1))


def _params(semantics, vmem=VMEM_LIMIT_BYTES):
    return pltpu.CompilerParams(dimension_semantics=semantics, vmem_limit_bytes=vmem)


def _interleave(major, minor):
    merged, done = [], 0
    for i, piece in enumerate(major):
        while done < len(minor) and done * len(major) <= i * len(minor):
            merged.append(minor[done])
            done += 1
        merged.append(piece)
    return merged + list(minor[done:])


def _ada_kernel(c_ref, w_ref, b_ref, o_ref):
    s = _silu(c_ref[...]).astype(BF16)
    o_ref[...] = jnp.dot(s, w_ref[...].astype(BF16), preferred_element_type=F32) + b_ref[...]


def _ada(cc, w, b, tn=1536):
    m, d = cc.shape
    n = w.shape[1]
    return pl.pallas_call(
        _ada_kernel,
        grid=(n // tn,),
        in_specs=[pl.BlockSpec((m, d), lambda j: (0, 0)),
                  pl.BlockSpec((d, tn), lambda j: (0, j)),
                  pl.BlockSpec((1, tn), lambda j: (0, j))],
        out_specs=pl.BlockSpec((m, tn), lambda j: (0, j)),
        out_shape=jax.ShapeDtypeStruct((m, n), F32),
        compiler_params=_params(("arbitrary",)),
        name="adaln",
    )(cc, w, b)


def _ffn_pre(x, sh, sc, gpre):
    return (_rms(x, gpre) * (1.0 + sc) + sh).astype(BF16)


def _ffn_hidden(h, wi_ref, act_ref, chunk, between=None):
    dff = act_ref.shape[1]
    for i, c in enumerate(range(0, dff, chunk)):
        gate = jnp.dot(h[...], wi_ref[:, c:c + chunk], preferred_element_type=F32)
        up = jnp.dot(h[...], wi_ref[:, dff + c:dff + c + chunk], preferred_element_type=F32)
        act_ref[:, c:c + chunk] = (_silu(gate) * up).astype(BF16)
        if between is not None:
            between(i, dff // chunk)


def _ffn_post(x, ga, gpost, act_ref, wo_ref):
    y = jnp.dot(act_ref[...], wo_ref[...], preferred_element_type=F32)
    return x + (0.5 * ga) * _rms(y, gpost)


def _ffn_body(x, sh, sc, ga, gpre, gpost, wi_ref, wo_ref, act_ref, chunk):
    _ffn_hidden(_ffn_pre(x, sh, sc, gpre), wi_ref, act_ref, chunk)
    return _ffn_post(x, ga, gpost, act_ref, wo_ref)


def _ffn_kernel(x_ref, sh_ref, sc_ref, ga_ref, gpre_ref, gpost_ref, wi_ref, wo_ref, *rest, chunk, n_cast):
    cast_in, o_ref, cast_out, act_ref = rest[:n_cast], rest[n_cast], rest[n_cast + 1:-1], rest[-1]
    o_ref[0] = _ffn_body(x_ref[0], sh_ref[0], sc_ref[0], ga_ref[0], gpre_ref[...], gpost_ref[...],
                         wi_ref, wo_ref, act_ref, chunk)
    for src, dst in zip(cast_in, cast_out):
        dst[...] = src[...].astype(dst.dtype)


def _ffn(x, mod, mod_row, k_shift, g_pre, g_post, wi, wo, casts=(), tm=512, chunk=MXU_DIM):
    bsz, seq, d = x.shape
    dff = wo.shape[0]
    nt = seq // tm
    n_steps = bsz * nt
    mod_spec = lambda k: pl.BlockSpec((1, 1, d), lambda b, i: (mod_row(b), 0, k))

    def cast_spec(w):
        rows = w.shape[0]
        n_blocks = max(n for n in range(1, n_steps + 1)
                       if n_steps % n == 0 and rows % n == 0 and (rows // n) % (2 * SUBLANES) == 0)
        rep = n_steps // n_blocks
        return pl.BlockSpec((rows // n_blocks, w.shape[1]), lambda b, i: ((b * nt + i) // rep, 0))

    cast_specs = lambda: [cast_spec(w) for w in casts]
    out = pl.pallas_call(
        functools.partial(_ffn_kernel, chunk=chunk, n_cast=len(casts)),
        grid=(bsz, nt),
        in_specs=[pl.BlockSpec((1, tm, d), lambda b, i: (b, i, 0)),
                  mod_spec(k_shift), mod_spec(k_shift + 1), mod_spec(k_shift + 2),
                  _resident((1, d)), _resident((1, d)),
                  _resident(wi.shape), _resident(wo.shape)] + cast_specs(),
        out_specs=[pl.BlockSpec((1, tm, d), lambda b, i: (b, i, 0))] + cast_specs(),
        out_shape=[jax.ShapeDtypeStruct(x.shape, F32)] + [jax.ShapeDtypeStruct(w.shape, BF16) for w in casts],
        scratch_shapes=[pltpu.VMEM((tm, dff), BF16)],
        compiler_params=_params(("arbitrary", "arbitrary")),
        name="ffn",
    )(x, mod, mod, mod, g_pre, g_post, wi, wo, *casts)
    return out if casts else out[0]


def _inproj_kernel(x_ref, sh_ref, sc_ref, g_ref, w_ref, *o_refs, splits, acts, chunk):
    h = (_rms(x_ref[0], g_ref[...]) * (1.0 + sc_ref[0]) + sh_ref[0]).astype(BF16)
    for o_ref, (start, width), act in zip(o_refs, splits, acts):
        for c in range(0, width, chunk):
            r = jnp.dot(h, w_ref[:, start + c:start + c + chunk], preferred_element_type=F32)
            if act is not None:
                r = act(r)
            o_ref[0, :, c:c + chunk] = r.astype(o_ref.dtype)


def _inproj(x, mod, mod_row, k_shift, g, w, splits, dtypes, acts, tm=512, chunk=512):
    bsz, seq, d = x.shape
    mod_spec = lambda k: pl.BlockSpec((1, 1, d), lambda b, i: (mod_row(b), 0, k))
    return pl.pallas_call(
        functools.partial(_inproj_kernel, splits=splits, acts=acts, chunk=chunk),
        grid=(bsz, seq // tm),
        in_specs=[pl.BlockSpec((1, tm, d), lambda b, i: (b, i, 0)),
                  mod_spec(k_shift), mod_spec(k_shift + 1),
                  _resident((1, d)), _resident(w.shape)],
        out_specs=[pl.BlockSpec((1, tm, wd), lambda b, i: (b, i, 0)) for _, wd in splits],
        out_shape=[jax.ShapeDtypeStruct((bsz, seq, wd), dt) for (_, wd), dt in zip(splits, dtypes)],
        compiler_params=_params(("arbitrary", "arbitrary")),
        name="inproj",
    )(x, mod, mod, g, w)


def _lru_c1(lam):
    e = jnp.exp(-jnp.abs(lam))
    u = 1.0 + e
    log1p_e = jnp.where(u == 1.0, e, jnp.log(u) * (e / (u - 1.0)))
    return (-0.5 * GATE_C * math.log2(math.e)) * (jnp.maximum(-lam, 0.0) + log1p_e)


def _lru_fill_ext(ext_ref, j, prev, main, nxt, at_start, at_end):
    tt = main.shape[0]
    ext_ref[j, 0:SUBLANES, :] = jnp.where(at_start, 0.0, prev)
    ext_ref[j, SUBLANES:SUBLANES + tt, :] = main
    ext_ref[j, SUBLANES + tt:2 * SUBLANES + tt, :] = jnp.where(at_end, 0.0, nxt)


def _lru_gates(ext_ref, a_ref, b_ref, j, row0, nrows, cw, cb, wg, br, bi, c1):
    _lru_gate_math(a_ref, b_ref, j, _lru_conv(ext_ref, j, row0, nrows, cw, cb), wg, br, bi, c1)


def _lru_conv(ext_ref, j, row0, nrows, cw, cb):
    xc = cb
    for k in range(CONV_W):
        off = SUBLANES - CONV_LEFT + k + row0
        tap = jnp.concatenate(
            [ext_ref[j, pl.ds(off + m + kk, SUBLANES, stride=SEG_LEN), :]
             for m in range(0, nrows, SCAN_ROWS) for kk in range(SEG_LEN)], axis=0)
        xc = xc + tap * cw[k:k + 1, :]
    return xc


def _lru_gate_math(a_ref, b_ref, j, xc, wg, br, bi, c1):
    z = jnp.dot(xc.astype(BF16), wg, preferred_element_type=F32)
    t_r = jnp.tanh(z[:, :LANES] + br)
    t_i = jnp.tanh(z[:, LANES:] + bi)
    a = jnp.exp2(c1 * t_r + c1)
    y = 1.0 - a * a
    root = jnp.where(y > 0.0, y * lax.rsqrt(y), 0.0)
    hx = 0.5 * xc
    a_ref[j] = a
    b_ref[j] = root * (t_i * hx + hx)


def _shift_rows(x, k, fill, rows):
    return jnp.where(rows >= k, pltpu.roll(x, k, axis=0), fill)


def _shift_rows_up(x, k, fill, rows):
    return jnp.where(rows < SUBLANES - k, pltpu.roll(x, SUBLANES - k, axis=0), fill)


def _scan_group(a_ref, b_ref, o_ref, j, base, out_base, carry, rows, reverse):
    ks = range(SEG_LEN - 1, -1, -1) if reverse else range(SEG_LEN)
    shift = _shift_rows_up if reverse else _shift_rows
    a = {k: a_ref[j, pl.ds(base + k * SUBLANES, SUBLANES), :] for k in ks}
    b = {k: b_ref[j, pl.ds(base + k * SUBLANES, SUBLANES), :] for k in ks}
    h, p = {}, {}
    prev = None
    for k in ks:
        if prev is None:
            h[k], p[k] = b[k], a[k]
        else:
            h[k] = a[k] * h[prev] + b[k]
            p[k] = a[k] * p[prev]
        prev = k
    hc, pc = h[prev], p[prev]
    for dd in (1, 2, 4):
        hc = pc * shift(hc, dd, 0.0, rows) + hc
        pc = pc * shift(pc, dd, 1.0, rows)
    g = hc + pc * carry
    cin = shift(g, 1, carry, rows)
    for k in ks:
        o_ref[j, pl.ds(out_base + k, SUBLANES, stride=SEG_LEN), :] = h[k] + p[k] * cin
    last = 0 if reverse else SUBLANES - 1
    return jnp.broadcast_to(g[last:last + 1, :], (SUBLANES, LANES))


def _rec_kernel(l_ref, lp_ref, ln_ref, cw_ref, cb_ref, wg_ref, br_ref, bi_ref, lam_ref, h0_ref,
                o_ref, ext_ref, a_ref, b_ref, ho_ref, carry_ref, *, tt, nc):
    d = pl.program_id(0)
    c = pl.program_id(2)
    tc = c + d * (nc - 1 - 2 * c)
    n_slab = a_ref.shape[0]
    c1 = _lru_c1(lam_ref[0])

    for j in range(n_slab):
        sl = slice(j * LANES, (j + 1) * LANES)
        _lru_fill_ext(ext_ref, j, lp_ref[0, :, sl], l_ref[0, :, sl], ln_ref[0, :, sl], tc == 0, tc == nc - 1)
        _lru_gates(ext_ref, a_ref, b_ref, j, 0, tt, cw_ref[:, sl], cb_ref[:, sl], wg_ref[0, j],
                   br_ref[0, :, sl], bi_ref[0, :, sl], c1[:, sl])

    @pl.when(c == 0)
    def _():
        for j in range(n_slab):
            carry_ref[j] = jnp.broadcast_to(h0_ref[0, 0, :, j * LANES:(j + 1) * LANES], (SUBLANES, LANES))

    rows = lax.broadcasted_iota(jnp.int32, (SUBLANES, LANES), 0)
    n_groups = tt // SCAN_ROWS

    def run(reverse):
        def body(m, carries):
            base = pl.multiple_of(((n_groups - 1 - m) if reverse else m) * SCAN_ROWS, SCAN_ROWS)
            return tuple(_scan_group(a_ref, b_ref, ho_ref, j, base, base, carries[j], rows, reverse)
                         for j in range(n_slab))
        carries = lax.fori_loop(0, n_groups, body, tuple(carry_ref[j] for j in range(n_slab)))
        for j in range(n_slab):
            carry_ref[j] = carries[j]

    @pl.when(d == 0)
    def _():
        run(False)

    @pl.when(d == 1)
    def _():
        run(True)

    for j in range(n_slab):
        o_ref[0, 0, :, j * LANES:(j + 1) * LANES] = ho_ref[j].astype(o_ref.dtype)


def _halo_specs(tt, t, ch, idx):
    hb = tt // SUBLANES
    last = t // SUBLANES - 1

    def prev_map(*g):
        b, tc = idx(*g)
        return b, jnp.maximum(tc * hb - 1, 0), 0

    def next_map(*g):
        b, tc = idx(*g)
        return b, jnp.minimum((tc + 1) * hb, last), 0

    return [pl.BlockSpec((1, tt, ch), lambda *g: idx(*g) + (0,)),
            pl.BlockSpec((1, SUBLANES, ch), prev_map),
            pl.BlockSpec((1, SUBLANES, ch), next_map)]


def _rglru(l, conv_w, conv_b, wg, br, bi, lam, h0, n_dir, tt=256):
    bsz, t, ch = l.shape
    nc = t // tt
    n_slab = ch // LANES
    tcf = lambda d, c: c + d * (nc - 1 - 2 * c)
    vec = lambda: pl.BlockSpec((1, 1, ch), lambda d, b, c: (d, 0, 0))
    return pl.pallas_call(
        functools.partial(_rec_kernel, tt=tt, nc=nc),
        grid=(n_dir, bsz, nc),
        in_specs=_halo_specs(tt, t, ch, lambda d, b, c: (b, tcf(d, c))) + [
            pl.BlockSpec(conv_w.shape, lambda d, b, c: (0, 0)),
            pl.BlockSpec((1, ch), lambda d, b, c: (0, 0)),
            pl.BlockSpec((1,) + wg.shape[1:], lambda d, b, c: (d, 0, 0, 0)),
            vec(), vec(), vec(),
            pl.BlockSpec((1, 1, 1, ch), lambda d, b, c: (d, b, 0, 0))],
        out_specs=pl.BlockSpec((1, 1, tt, ch), lambda d, b, c: (d, b, tcf(d, c), 0)),
        out_shape=jax.ShapeDtypeStruct((n_dir, bsz, t, ch), F32),
        scratch_shapes=[pltpu.VMEM((n_slab, tt + 2 * SUBLANES, LANES), F32),
                        pltpu.VMEM((n_slab, tt, LANES), F32),
                        pltpu.VMEM((n_slab, tt, LANES), F32),
                        pltpu.VMEM((n_slab, tt, LANES), F32),
                        pltpu.VMEM((n_slab, SUBLANES, LANES), F32)],
        compiler_params=_params(("arbitrary", "arbitrary", "arbitrary")),
        name="rglru",
    )(l, l, l, conv_w, conv_b, wg, br, bi, lam, h0)


def _dft_tables(rows, cols, group_w, groups):
    n_tok = rows * cols
    scale = 1.0 / math.sqrt(n_tok * group_w)
    ch = np.arange(group_w)
    ph = 2.0 * np.pi * ((ch[:, None] * ch[None, :]) % group_w) / group_w
    d_f = groups * group_w
    d2 = np.zeros((2 * d_f, d_f), np.float64)
    for gi in range(groups):
        s = slice(gi * group_w, (gi + 1) * group_w)
        d2[s, s] = np.cos(ph)
        d2[d_f + gi * group_w:d_f + (gi + 1) * group_w, s] = np.sin(ph)
    t = np.arange(n_tok)
    r, c = t // cols, t % cols
    th = 2.0 * np.pi * (((r[:, None] * r[None, :]) % rows) / rows + ((c[:, None] * c[None, :]) % cols) / cols)
    wc = np.cos(th) * scale
    ws = -np.sin(th) * scale
    as_bf16 = lambda m: jnp.asarray(m.astype(np.float32)).astype(BF16)
    return as_bf16(wc), as_bf16(ws), as_bf16(d2)


def _fourier_kernel(f_ref, wc_ref, ws_ref, d2_ref, o_ref, *, tm):
    n_tok, df = f_ref.shape[1], f_ref.shape[2]
    for m in range(0, n_tok, tm):
        p = jnp.dot(wc_ref[m:m + tm, :], f_ref[0], preferred_element_type=F32).astype(BF16)
        q = jnp.dot(ws_ref[m:m + tm, :], f_ref[0], preferred_element_type=F32).astype(BF16)
        o = jnp.dot(p, d2_ref[:df, :], preferred_element_type=F32)
        o = o + jnp.dot(q, d2_ref[df:, :], preferred_element_type=F32)
        o_ref[0, m:m + tm, :] = o.astype(o_ref.dtype)


def _fourier(f, wc, ws, d2, tm=512):
    bsz, n_tok, df = f.shape
    return pl.pallas_call(
        functools.partial(_fourier_kernel, tm=tm),
        grid=(bsz,),
        in_specs=[pl.BlockSpec((1, n_tok, df), lambda b: (b, 0, 0)),
                  _resident(wc.shape), _resident(ws.shape), _resident(d2.shape)],
        out_specs=pl.BlockSpec((1, n_tok, df), lambda b: (b, 0, 0)),
        out_shape=jax.ShapeDtypeStruct(f.shape, BF16),
        compiler_params=_params(("arbitrary",)),
        name="fourier",
    )(f, wc, ws, d2)


def _mid_kernel(x_ref, sh_ref, sc_ref, g_ref, w_ref, cw_ref, cb_ref, wg_ref, br_ref, bi_ref, lam_ref, h0_ref,
                f_ref, l_ref, gg_ref, ga_ref, gb_ref, hf_ref, ext_ref, a_ref, b_ref, ho_ref, carry_ref,
                *, tm, nt, n_tiles, half, chunk, splits):
    s = pl.program_id(0)
    ip = jnp.minimum(s, n_tiles - 1) % nt
    ir = jnp.maximum(s - 1, 0) % nt
    n_slab = a_ref.shape[0]
    (f0, fw), (l0, lw), (g0, gw), (ga0, gaw), (gb0, gbw) = splits

    @pl.when(s == 0)
    def _():
        ext_ref[...] = jnp.zeros(ext_ref.shape, F32)
        carry_ref[...] = jnp.zeros(carry_ref.shape, F32)

    h = (_rms(x_ref[0], g_ref[...]) * (1.0 + sc_ref[0]) + sh_ref[0]).astype(BF16)

    def project(o_ref, start, c, act):
        r = jnp.dot(h, w_ref[:, start + c:start + c + chunk], preferred_element_type=F32)
        o_ref[0, :, c:c + chunk] = (r if act is None else act(r)).astype(o_ref.dtype)

    c1 = _lru_c1(lam_ref[...])
    rows = lax.broadcasted_iota(jnp.int32, (SUBLANES, LANES), 0)
    carries = []
    for j in range(n_slab):
        sl = slice(j * LANES, (j + 1) * LANES)
        h0 = jnp.broadcast_to(h0_ref[0, :, sl], (SUBLANES, LANES))
        carries.append(jnp.where(ir == 0, h0, carry_ref[j]))

    def close_window():
        for j in range(n_slab):
            sl = slice(j * LANES, (j + 1) * LANES)
            ext_ref[j, SUBLANES + tm:2 * SUBLANES + tm, :] = jnp.where(ir == nt - 1, 0.0, l_ref[0, 0:SUBLANES, sl])

    xcs = {}

    def conv_piece(row0, j):
        sl = slice(j * LANES, (j + 1) * LANES)
        xcs[row0, j] = _lru_conv(ext_ref, j, row0, half, cw_ref[:, sl], cb_ref[:, sl])

    def gate_piece(row0, j):
        sl = slice(j * LANES, (j + 1) * LANES)
        _lru_gate_math(a_ref, b_ref, j, xcs.pop((row0, j)), wg_ref[j], br_ref[:, sl], bi_ref[:, sl], c1[:, sl])

    def scan_piece(row0, base):
        carries[:] = [_scan_group(a_ref, b_ref, ho_ref, j, base, row0 + base, carries[j], rows, False)
                      for j in range(n_slab)]

    scan_pieces = []
    for row0 in range(0, tm, half):
        if row0 + half == tm:
            scan_pieces.append(close_window)
        scan_pieces.append(functools.partial(conv_piece, row0, 0))
        for j in range(n_slab):
            if j + 1 < n_slab:
                scan_pieces.append(functools.partial(conv_piece, row0, j + 1))
            scan_pieces.append(functools.partial(gate_piece, row0, j))
        scan_pieces += [functools.partial(scan_piece, row0, base) for base in range(0, half, SCAN_ROWS)]

    proj_pieces = [functools.partial(project, o_ref, start, c, act)
                   for o_ref, start, width, act in ((l_ref, l0, lw, None), (f_ref, f0, fw, None),
                                                    (gg_ref, g0, gw, _gelu_tanh), (ga_ref, ga0, gaw, _sigmoid),
                                                    (gb_ref, gb0, gbw, _sigmoid))
                   for c in range(0, width, chunk)]
    assert half < tm and lw // chunk <= len(proj_pieces) // 2
    for piece in _interleave(scan_pieces, proj_pieces):
        piece()

    for j in range(n_slab):
        sl = slice(j * LANES, (j + 1) * LANES)
        carry_ref[j] = carries[j]
        hf_ref[0, 0, :, sl] = ho_ref[j].astype(hf_ref.dtype)
        ext_ref[j, 0:SUBLANES, :] = jnp.where(ip == 0, 0.0, ext_ref[j, tm:tm + SUBLANES, :])
        ext_ref[j, SUBLANES:SUBLANES + tm, :] = l_ref[0, :, sl]


def _mid(x, mod, k_shift, g, w, splits, conv_w, conv_b, wg, br, bi, lam, h0, tm=512, half=256, chunk=MXU_DIM):
    bsz, seq, d = x.shape
    ch = splits[1][1]
    nt = seq // tm
    n_tiles = bsz * nt
    n_slab = ch // LANES

    def proj_tile(s):
        p = jnp.minimum(s, n_tiles - 1)
        return p // nt, p % nt

    def scan_tile(s):
        r = jnp.maximum(s - 1, 0)
        return r // nt, r % nt

    tok = lambda wd: pl.BlockSpec((1, tm, wd), lambda s: proj_tile(s) + (0,))
    mod_spec = lambda k: pl.BlockSpec((1, 1, d), lambda s: (proj_tile(s)[0], 0, k))
    widths = [wd for _, wd in splits]
    return pl.pallas_call(
        functools.partial(_mid_kernel, tm=tm, nt=nt, n_tiles=n_tiles, half=half, chunk=chunk, splits=splits),
        grid=(n_tiles + 1,),
        in_specs=[tok(d), mod_spec(k_shift), mod_spec(k_shift + 1), _resident((1, d)), _resident(w.shape),
                  _resident(conv_w.shape), _resident((1, ch)), _resident(wg.shape),
                  _resident((1, ch)), _resident((1, ch)), _resident((1, ch)),
                  pl.BlockSpec((1, 1, ch), lambda s: (scan_tile(s)[0], 0, 0))],
        out_specs=[tok(wd) for wd in widths] + [
            pl.BlockSpec((1, 1, tm, ch), lambda s: (0,) + scan_tile(s) + (0,))],
        out_shape=[jax.ShapeDtypeStruct((bsz, seq, wd), dt)
                   for wd, dt in zip(widths, (BF16, F32, BF16, BF16, BF16))] + [
            jax.ShapeDtypeStruct((1, bsz, seq, ch), BF16)],
        scratch_shapes=[pltpu.VMEM((n_slab, tm + 2 * SUBLANES, LANES), F32),
                        pltpu.VMEM((n_slab, half, LANES), F32),
                        pltpu.VMEM((n_slab, half, LANES), F32),
                        pltpu.VMEM((n_slab, tm, LANES), F32),
                        pltpu.VMEM((n_slab, SUBLANES, LANES), F32)],
        compiler_params=_params(("arbitrary",)),
        name="mid",
    )(x, mod, mod, g, w, conv_w, conv_b, wg, br, bi, lam, h0)


def _tail_kernel(l_ref, lp_ref, ln_ref, cw_ref, cb_ref, wg_ref, br_ref, bi_ref, lam_ref, h0_ref,
                 x_ref, hf_ref, g_ref, ga_ref, gb_ref, fo_ref,
                 gate2_ref, sh3_ref, sc3_ref, gate3_ref, gn_ref, gpre_ref, gpost_ref,
                 wfa_ref, wfb_ref, wo_ref, wi2_ref, wo2_ref,
                 o_ref, ext_ref, a_ref, b_ref, hb_ref, carry_ref, act_ref, *, tm, nt, n_tiles, half, chunk):
    s = pl.program_id(0)
    q = jnp.minimum(s, n_tiles - 1)
    cq = nt - 1 - q % nt
    n_slab = a_ref.shape[0]

    @pl.when(s == 0)
    def _():
        hb_ref[...] = jnp.zeros(hb_ref.shape, F32)
        carry_ref[...] = jnp.zeros(carry_ref.shape, F32)

    hb = jnp.concatenate([hb_ref[j] for j in range(n_slab)], axis=1)
    u = ((hf_ref[0, 0].astype(F32) + hb) * g_ref[0].astype(F32)).astype(BF16)
    yb = jnp.dot(u, wfb_ref[...], preferred_element_type=F32)
    ya = jnp.dot(fo_ref[0], wfa_ref[...], preferred_element_type=F32)
    m = ga_ref[0].astype(F32) * ya + gb_ref[0].astype(F32) * yb
    mx = jnp.dot(m.astype(BF16), wo_ref[...], preferred_element_type=F32)
    x2 = x_ref[0] + gate2_ref[0] * _rms(mx, gn_ref[...])

    c1 = _lru_c1(lam_ref[...])
    rows = lax.broadcasted_iota(jnp.int32, (SUBLANES, LANES), 0)
    carries = []
    for j in range(n_slab):
        sl = slice(j * LANES, (j + 1) * LANES)
        _lru_fill_ext(ext_ref, j, lp_ref[0, :, sl], l_ref[0, :, sl], ln_ref[0, :, sl], cq == 0, cq == nt - 1)
        h0 = jnp.broadcast_to(h0_ref[0, :, sl], (SUBLANES, LANES))
        carries.append(jnp.where(cq == nt - 1, h0, carry_ref[j]))

    xcs = {}

    def conv_piece(row0, j):
        sl = slice(j * LANES, (j + 1) * LANES)
        xcs[row0, j] = _lru_conv(ext_ref, j, row0, half, cw_ref[:, sl], cb_ref[:, sl])

    def gate_piece(row0, j):
        sl = slice(j * LANES, (j + 1) * LANES)
        _lru_gate_math(a_ref, b_ref, j, xcs.pop((row0, j)), wg_ref[j], br_ref[:, sl], bi_ref[:, sl], c1[:, sl])

    def scan_piece(row0, base):
        carries[:] = [_scan_group(a_ref, b_ref, hb_ref, j, base, row0 + base, carries[j], rows, True)
                      for j in range(n_slab)]

    pieces = []
    for row0 in range(tm - half, -1, -half):
        pieces.append(functools.partial(conv_piece, row0, 0))
        for j in range(n_slab):
            if j + 1 < n_slab:
                pieces.append(functools.partial(conv_piece, row0, j + 1))
            pieces.append(functools.partial(gate_piece, row0, j))
        pieces += [functools.partial(scan_piece, row0, base) for base in range(half - SCAN_ROWS, -1, -SCAN_ROWS)]

    def between(i, n):
        for piece in pieces[i * len(pieces) // n:(i + 1) * len(pieces) // n]:
            piece()

    _ffn_hidden(_ffn_pre(x2, sh3_ref[0], sc3_ref[0], gpre_ref[...]), wi2_ref, act_ref, chunk, between)
    o_ref[0] = _ffn_post(x2, gate3_ref[0], gpost_ref[...], act_ref, wo2_ref)
    for j in range(n_slab):
        carry_ref[j] = carries[j]


def _tail(l, conv_w, conv_b, wg, br, bi, lam, h0, x, hf, g, ga, gb, fo, mod, gn, gpre, gpost,
          wfa, wfb, wo, wi2, wo2, tm=512, half=256, chunk=MXU_DIM):
    bsz, seq, d = x.shape
    ch = l.shape[2]
    nt = seq // tm
    n_tiles = bsz * nt
    n_slab = ch // LANES
    dff = wo2.shape[0]

    def scan_tile(s):
        q = jnp.minimum(s, n_tiles - 1)
        return q // nt, nt - 1 - q % nt

    def merge_tile(s):
        r = jnp.clip(s - 1, 0, n_tiles - 1)
        return r // nt, nt - 1 - r % nt

    tok = lambda w: pl.BlockSpec((1, tm, w), lambda s: merge_tile(s) + (0,))
    mod_spec = lambda k: pl.BlockSpec((1, 1, d), lambda s: (merge_tile(s)[0], 0, k))
    return pl.pallas_call(
        functools.partial(_tail_kernel, tm=tm, nt=nt, n_tiles=n_tiles, half=half, chunk=chunk),
        grid=(n_tiles + 1,),
        in_specs=_halo_specs(tm, seq, ch, scan_tile) + [
            _resident(conv_w.shape), _resident((1, ch)), _resident(wg.shape),
            _resident((1, ch)), _resident((1, ch)), _resident((1, ch)),
            pl.BlockSpec((1, 1, ch), lambda s: (scan_tile(s)[0], 0, 0)),
            tok(d),
            pl.BlockSpec((1, 1, tm, ch), lambda s: (0,) + merge_tile(s) + (0,)),
            tok(ch), tok(d), tok(d), tok(fo.shape[2]),
            mod_spec(5), mod_spec(6), mod_spec(7), mod_spec(8),
            _resident((1, d)), _resident((1, d)), _resident((1, d)),
            _resident(wfa.shape), _resident(wfb.shape), _resident(wo.shape),
            _resident(wi2.shape), _resident(wo2.shape)],
        out_specs=tok(d),
        out_shape=jax.ShapeDtypeStruct(x.shape, F32),
        scratch_shapes=[pltpu.VMEM((n_slab, tm + 2 * SUBLANES, LANES), F32),
                        pltpu.VMEM((n_slab, half, LANES), F32),
                        pltpu.VMEM((n_slab, half, LANES), F32),
                        pltpu.VMEM((n_slab, tm, LANES), F32),
                        pltpu.VMEM((n_slab, SUBLANES, LANES), F32),
                        pltpu.VMEM((tm, dff), BF16)],
        compiler_params=_params(("arbitrary",), TAIL_VMEM_LIMIT_BYTES),
        name="tail",
    )(l, l, l, conv_w, conv_b, wg, br, bi, lam, h0,
      x, hf, g, ga, gb, fo, mod, mod, mod, mod, gn, gpre, gpost, wfa, wfb, wo, wi2, wo2)


def kernel(x, c, ctx, c_ctx, w_ada, b_ada, norm_g, w_ffn1_in, w_ffn1_out, w_ffn2_in, w_ffn2_out,
           w_in, conv_w, conv_b, w_r, b_r, w_i, b_i, lam, w_fa, w_fb, w_out):
    depth = w_ada.shape[0]
    assert depth == 1, "single-layer problem: the context stream is only needed up to the mixer scans"
    bsz, seq, d = x.shape
    d_lru = w_fb.shape[1]
    d_f = w_fa.shape[1]
    lyr = 0

    pad = (-(bsz + 1)) % SUBLANES
    cc = jnp.concatenate([c, c_ctx[None, :], jnp.zeros((pad, d), F32)], axis=0)
    mod = _ada(cc, w_ada[lyr], b_ada[lyr][None, :])
    mod = mod.reshape(mod.shape[0], 1, N_MOD * d)
    lat_row = lambda b: b
    ctx_row = lambda b: bsz
    g = norm_g[lyr][:, None, :]

    wi1, wo1 = w_ffn1_in[lyr].astype(BF16), w_ffn1_out[lyr].astype(BF16)

    x1, wi2, wo2, w_in_b, wfa_b, wfb_b, wout_b = _ffn(
        x, mod, lat_row, 0, g[0], g[1], wi1, wo1,
        casts=(w_ffn2_in[lyr], w_ffn2_out[lyr], w_in[lyr], w_fa[lyr], w_fb[lyr], w_out[lyr]), tm=1024)
    ctx_flat = ctx.reshape(1, bsz * ctx.shape[1], d)
    c1 = _ffn(ctx_flat, mod, ctx_row, 0, g[0], g[1], wi1, wo1)

    (lc,) = _inproj(c1, mod, ctx_row, 3, g[2], w_in_b[:, d_f:d_f + d_lru], ((0, d_lru),), (F32,), (None,))
    lc = lc.reshape(bsz, ctx.shape[1], d_lru)

    wgate = (0.5 * jnp.concatenate([w_r[lyr], w_i[lyr]], axis=-1)).astype(BF16)
    br, bi, lm = 0.5 * b_r[lyr][:, None, :], 0.5 * b_i[lyr][:, None, :], lam[lyr][:, None, :]
    cw, cb = conv_w[lyr], conv_b[lyr][None, :]
    h_ctx = _rglru(lc, cw, cb, wgate, br, bi, lm, jnp.zeros((2, bsz, 1, d_lru), F32), 2, tt=ctx.shape[1])
    h0f = h_ctx[0, :, -1][:, None, :]
    h0b = h_ctx[1, :, 0][:, None, :]

    splits = ((0, d_f), (d_f, d_lru), (d_f + d_lru, d_lru), (d_f + 2 * d_lru, d), (d_f + 2 * d_lru + d, d))
    fx, lx, gx, gax, gbx, hf = _mid(x1, mod, 3, g[2], w_in_b, splits, cw, cb, wgate[0], br[0], bi[0], lm[0], h0f)

    wc, ws, d2 = _dft_tables(seq // GRID_W, GRID_W, d_f // FOURIER_GROUPS, FOURIER_GROUPS)
    fo = _fourier(fx, wc, ws, d2)

    return _tail(lx, cw, cb, wgate[1], br[1], bi[1], lm[1], h0b, x1, hf, gx, gax, gbx, fo, mod,
                 g[3], g[4], g[5], wfa_b, wfb_b, wout_b, wi2, wo2)
```

```python
import functools
import math

import numpy as np
import jax
import jax.numpy as jnp
from jax import lax
from jax.experimental import pallas as pl
from jax.experimental.pallas import tpu as pltpu

LANES = 128
SUBLANES = 8
MXU_DIM = 256
VMEM_LIMIT_BYTES = 56 * 1024 * 1024
TAIL_VMEM_LIMIT_BYTES = 62 * 1024 * 1024

GRID_W = 64
FOURIER_GROUPS = 4
CONV_W = 4
CONV_LEFT = (CONV_W - 1) // 2
GATE_C = 8.0
N_MOD = 9
EPS = 1e-6

BF16 = jnp.bfloat16
F32 = jnp.float32

SEG_LEN = 4
SCAN_ROWS = SEG_LEN * SUBLANES


def _sigmoid(x):
    return 0.5 * jnp.tanh(0.5 * x) + 0.5


def _silu(x):
    return x * _sigmoid(x)


def _gelu_tanh(x):
    c = math.sqrt(2.0 / math.pi)
    return 0.5 * x * (1.0 + jnp.tanh(c * (x + 0.044715 * (x * x * x))))


def _rms(x, g):
    return x * lax.rsqrt(jnp.mean(x * x, axis=-1, keepdims=True) + EPS) * g


def _resident(shape):
    nd = len(shape)
    return pl.BlockSpec(shape, lambda *_: (0,) * nd, pipeline_mode=pl.Buffered(1))


def _params(semantics, vmem=VMEM_LIMIT_BYTES):
    return pltpu.CompilerParams(dimension_semantics=semantics, vmem_limit_bytes=vmem)


def _interleave(major, minor):
    merged, done = [], 0
    for i, piece in enumerate(major):
        while done < len(minor) and done * len(major) <= i * len(minor):
            merged.append(minor[done])
            done += 1
        merged.append(piece)
    return merged + list(minor[done:])


def _ada_kernel(c_ref, w_ref, b_ref, o_ref):
    s = _silu(c_ref[...]).astype(BF16)
    o_ref[...] = jnp.dot(s, w_ref[...].astype(BF16), preferred_element_type=F32) + b_ref[...]


def _ada(cc, w, b, tn=1536):
    m, d = cc.shape
    n = w.shape[1]
    return pl.pallas_call(
        _ada_kernel,
        grid=(n // tn,),
        in_specs=[pl.BlockSpec((m, d), lambda j: (0, 0)),
                  pl.BlockSpec((d, tn), lambda j: (0, j)),
                  pl.BlockSpec((1, tn), lambda j: (0, j))],
        out_specs=pl.BlockSpec((m, tn), lambda j: (0, j)),
        out_shape=jax.ShapeDtypeStruct((m, n), F32),
        compiler_params=_params(("arbitrary",)),
        name="adaln",
    )(cc, w, b)


def _ffn_pre(x, sh, sc, gpre):
    return (_rms(x, gpre) * (1.0 + sc) + sh).astype(BF16)


def _ffn_hidden(h, wi_ref, act_ref, chunk, between=None):
    dff = act_ref.shape[1]
    for i, c in enumerate(range(0, dff, chunk)):
        gate = jnp.dot(h[...], wi_ref[:, c:c + chunk], preferred_element_type=F32)
        up = jnp.dot(h[...], wi_ref[:, dff + c:dff + c + chunk], preferred_element_type=F32)
        act_ref[:, c:c + chunk] = (_silu(gate) * up).astype(BF16)
        if between is not None:
            between(i, dff // chunk)


def _ffn_post(x, ga, gpost, act_ref, wo_ref):
    y = jnp.dot(act_ref[...], wo_ref[...], preferred_element_type=F32)
    return x + (0.5 * ga) * _rms(y, gpost)


def _ffn_body(x, sh, sc, ga, gpre, gpost, wi_ref, wo_ref, act_ref, chunk):
    _ffn_hidden(_ffn_pre(x, sh, sc, gpre), wi_ref, act_ref, chunk)
    return _ffn_post(x, ga, gpost, act_ref, wo_ref)


def _ffn_kernel(x_ref, sh_ref, sc_ref, ga_ref, gpre_ref, gpost_ref, wi_ref, wo_ref, *rest, chunk, n_cast):
    cast_in, o_ref, cast_out, act_ref = rest[:n_cast], rest[n_cast], rest[n_cast + 1:-1], rest[-1]
    o_ref[0] = _ffn_body(x_ref[0], sh_ref[0], sc_ref[0], ga_ref[0], gpre_ref[...], gpost_ref[...],
                         wi_ref, wo_ref, act_ref, chunk)
    for src, dst in zip(cast_in, cast_out):
        dst[...] = src[...].astype(dst.dtype)


def _ffn(x, mod, mod_row, k_shift, g_pre, g_post, wi, wo, casts=(), tm=512, chunk=MXU_DIM):
    bsz, seq, d = x.shape
    dff = wo.shape[0]
    nt = seq // tm
    n_steps = bsz * nt
    mod_spec = lambda k: pl.BlockSpec((1, 1, d), lambda b, i: (mod_row(b), 0, k))

    def cast_spec(w):
        rows = w.shape[0]
        n_blocks = max(n for n in range(1, n_steps + 1)
                       if n_steps % n == 0 and rows % n == 0 and (rows // n) % (2 * SUBLANES) == 0)
        rep = n_steps // n_blocks
        return pl.BlockSpec((rows // n_blocks, w.shape[1]), lambda b, i: ((b * nt + i) // rep, 0))

    cast_specs = lambda: [cast_spec(w) for w in casts]
    out = pl.pallas_call(
        functools.partial(_ffn_kernel, chunk=chunk, n_cast=len(casts)),
        grid=(bsz, nt),
        in_specs=[pl.BlockSpec((1, tm, d), lambda b, i: (b, i, 0)),
                  mod_spec(k_shift), mod_spec(k_shift + 1), mod_spec(k_shift + 2),
                  _resident((1, d)), _resident((1, d)),
                  _resident(wi.shape), _resident(wo.shape)] + cast_specs(),
        out_specs=[pl.BlockSpec((1, tm, d), lambda b, i: (b, i, 0))] + cast_specs(),
        out_shape=[jax.ShapeDtypeStruct(x.shape, F32)] + [jax.ShapeDtypeStruct(w.shape, BF16) for w in casts],
        scratch_shapes=[pltpu.VMEM((tm, dff), BF16)],
        compiler_params=_params(("arbitrary", "arbitrary")),
        name="ffn",
    )(x, mod, mod, mod, g_pre, g_post, wi, wo, *casts)
    return out if casts else out[0]


def _inproj_kernel(x_ref, sh_ref, sc_ref, g_ref, w_ref, *o_refs, splits, acts, chunk):
    h = (_rms(x_ref[0], g_ref[...]) * (1.0 + sc_ref[0]) + sh_ref[0]).astype(BF16)
    for o_ref, (start, width), act in zip(o_refs, splits, acts):
        for c in range(0, width, chunk):
            r = jnp.dot(h, w_ref[:, start + c:start + c + chunk], preferred_element_type=F32)
            if act is not None:
                r = act(r)
            o_ref[0, :, c:c + chunk] = r.astype(o_ref.dtype)


def _inproj(x, mod, mod_row, k_shift, g, w, splits, dtypes, acts, tm=512, chunk=512):
    bsz, seq, d = x.shape
    mod_spec = lambda k: pl.BlockSpec((1, 1, d), lambda b, i: (mod_row(b), 0, k))
    return pl.pallas_call(
        functools.partial(_inproj_kernel, splits=splits, acts=acts, chunk=chunk),
        grid=(bsz, seq // tm),
        in_specs=[pl.BlockSpec((1, tm, d), lambda b, i: (b, i, 0)),
                  mod_spec(k_shift), mod_spec(k_shift + 1),
                  _resident((1, d)), _resident(w.shape)],
        out_specs=[pl.BlockSpec((1, tm, wd), lambda b, i: (b, i, 0)) for _, wd in splits],
        out_shape=[jax.ShapeDtypeStruct((bsz, seq, wd), dt) for (_, wd), dt in zip(splits, dtypes)],
        compiler_params=_params(("arbitrary", "arbitrary")),
        name="inproj",
    )(x, mod, mod, g, w)


def _lru_c1(lam):
    e = jnp.exp(-jnp.abs(lam))
    u = 1.0 + e
    log1p_e = jnp.where(u == 1.0, e, jnp.log(u) * (e / (u - 1.0)))
    return (-0.5 * GATE_C * math.log2(math.e)) * (jnp.maximum(-lam, 0.0) + log1p_e)


def _lru_fill_ext(ext_ref, j, prev, main, nxt, at_start, at_end):
    tt = main.shape[0]
    ext_ref[j, 0:SUBLANES, :] = jnp.where(at_start, 0.0, prev)
    ext_ref[j, SUBLANES:SUBLANES + tt, :] = main
    ext_ref[j, SUBLANES + tt:2 * SUBLANES + tt, :] = jnp.where(at_end, 0.0, nxt)


def _lru_gates(ext_ref, a_ref, b_ref, j, row0, nrows, cw, cb, wg, br, bi, c1):
    _lru_gate_math(a_ref, b_ref, j, _lru_conv(ext_ref, j, row0, nrows, cw, cb), wg, br, bi, c1)


def _lru_conv(ext_ref, j, row0, nrows, cw, cb):
    xc = cb
    for k in range(CONV_W):
        off = SUBLANES - CONV_LEFT + k + row0
        tap = jnp.concatenate(
            [ext_ref[j, pl.ds(off + m + kk, SUBLANES, stride=SEG_LEN), :]
             for m in range(0, nrows, SCAN_ROWS) for kk in range(SEG_LEN)], axis=0)
        xc = xc + tap * cw[k:k + 1, :]
    return xc


def _lru_gate_math(a_ref, b_ref, j, xc, wg, br, bi, c1):
    z = jnp.dot(xc.astype(BF16), wg, preferred_element_type=F32)
    t_r = jnp.tanh(z[:, :LANES] + br)
    t_i = jnp.tanh(z[:, LANES:] + bi)
    a = jnp.exp2(c1 * t_r + c1)
    y = 1.0 - a * a
    root = jnp.where(y > 0.0, y * lax.rsqrt(y), 0.0)
    hx = 0.5 * xc
    a_ref[j] = a
    b_ref[j] = root * (t_i * hx + hx)


def _shift_rows(x, k, fill, rows):
    return jnp.where(rows >= k, pltpu.roll(x, k, axis=0), fill)


def _shift_rows_up(x, k, fill, rows):
    return jnp.where(rows < SUBLANES - k, pltpu.roll(x, SUBLANES - k, axis=0), fill)


def _scan_group(a_ref, b_ref, o_ref, j, base, out_base, carry, rows, reverse):
    ks = range(SEG_LEN - 1, -1, -1) if reverse else range(SEG_LEN)
    shift = _shift_rows_up if reverse else _shift_rows
    a = {k: a_ref[j, pl.ds(base + k * SUBLANES, SUBLANES), :] for k in ks}
    b = {k: b_ref[j, pl.ds(base + k * SUBLANES, SUBLANES), :] for k in ks}
    h, p = {}, {}
    prev = None
    for k in ks:
        if prev is None:
            h[k], p[k] = b[k], a[k]
        else:
            h[k] = a[k] * h[prev] + b[k]
            p[k] = a[k] * p[prev]
        prev = k
    hc, pc = h[prev], p[prev]
    for dd in (1, 2, 4):
        hc = pc * shift(hc, dd, 0.0, rows) + hc
        pc = pc * shift(pc, dd, 1.0, rows)
    g = hc + pc * carry
    cin = shift(g, 1, carry, rows)
    for k in ks:
        o_ref[j, pl.ds(out_base + k, SUBLANES, stride=SEG_LEN), :] = h[k] + p[k] * cin
    last = 0 if reverse else SUBLANES - 1
    return jnp.broadcast_to(g[last:last + 1, :], (SUBLANES, LANES))


def _rec_kernel(l_ref, lp_ref, ln_ref, cw_ref, cb_ref, wg_ref, br_ref, bi_ref, lam_ref, h0_ref,
                o_ref, ext_ref, a_ref, b_ref, ho_ref, carry_ref, *, tt, nc):
    d = pl.program_id(0)
    c = pl.program_id(2)
    tc = c + d * (nc - 1 - 2 * c)
    n_slab = a_ref.shape[0]
    c1 = _lru_c1(lam_ref[0])

    for j in range(n_slab):
        sl = slice(j * LANES, (j + 1) * LANES)
        _lru_fill_ext(ext_ref, j, lp_ref[0, :, sl], l_ref[0, :, sl], ln_ref[0, :, sl], tc == 0, tc == nc - 1)
        _lru_gates(ext_ref, a_ref, b_ref, j, 0, tt, cw_ref[:, sl], cb_ref[:, sl], wg_ref[0, j],
                   br_ref[0, :, sl], bi_ref[0, :, sl], c1[:, sl])

    @pl.when(c == 0)
    def _():
        for j in range(n_slab):
            carry_ref[j] = jnp.broadcast_to(h0_ref[0, 0, :, j * LANES:(j + 1) * LANES], (SUBLANES, LANES))

    rows = lax.broadcasted_iota(jnp.int32, (SUBLANES, LANES), 0)
    n_groups = tt // SCAN_ROWS

    def run(reverse):
        def body(m, carries):
            base = pl.multiple_of(((n_groups - 1 - m) if reverse else m) * SCAN_ROWS, SCAN_ROWS)
            return tuple(_scan_group(a_ref, b_ref, ho_ref, j, base, base, carries[j], rows, reverse)
                         for j in range(n_slab))
        carries = lax.fori_loop(0, n_groups, body, tuple(carry_ref[j] for j in range(n_slab)))
        for j in range(n_slab):
            carry_ref[j] = carries[j]

    @pl.when(d == 0)
    def _():
        run(False)

    @pl.when(d == 1)
    def _():
        run(True)

    for j in range(n_slab):
        o_ref[0, 0, :, j * LANES:(j + 1) * LANES] = ho_ref[j].astype(o_ref.dtype)


def _halo_specs(tt, t, ch, idx):
    hb = tt // SUBLANES
    last = t // SUBLANES - 1

    def prev_map(*g):
        b, tc = idx(*g)
        return b, jnp.maximum(tc * hb - 1, 0), 0

    def next_map(*g):
        b, tc = idx(*g)
        return b, jnp.minimum((tc + 1) * hb, last), 0

    return [pl.BlockSpec((1, tt, ch), lambda *g: idx(*g) + (0,)),
            pl.BlockSpec((1, SUBLANES, ch), prev_map),
            pl.BlockSpec((1, SUBLANES, ch), next_map)]


def _rglru(l, conv_w, conv_b, wg, br, bi, lam, h0, n_dir, tt=256):
    bsz, t, ch = l.shape
    nc = t // tt
    n_slab = ch // LANES
    tcf = lambda d, c: c + d * (nc - 1 - 2 * c)
    vec = lambda: pl.BlockSpec((1, 1, ch), lambda d, b, c: (d, 0, 0))
    return pl.pallas_call(
        functools.partial(_rec_kernel, tt=tt, nc=nc),
        grid=(n_dir, bsz, nc),
        in_specs=_halo_specs(tt, t, ch, lambda d, b, c: (b, tcf(d, c))) + [
            pl.BlockSpec(conv_w.shape, lambda d, b, c: (0, 0)),
            pl.BlockSpec((1, ch), lambda d, b, c: (0, 0)),
            pl.BlockSpec((1,) + wg.shape[1:], lambda d, b, c: (d, 0, 0, 0)),
            vec(), vec(), vec(),
            pl.BlockSpec((1, 1, 1, ch), lambda d, b, c: (d, b, 0, 0))],
        out_specs=pl.BlockSpec((1, 1, tt, ch), lambda d, b, c: (d, b, tcf(d, c), 0)),
        out_shape=jax.ShapeDtypeStruct((n_dir, bsz, t, ch), F32),
        scratch_shapes=[pltpu.VMEM((n_slab, tt + 2 * SUBLANES, LANES), F32),
                        pltpu.VMEM((n_slab, tt, LANES), F32),
                        pltpu.VMEM((n_slab, tt, LANES), F32),
                        pltpu.VMEM((n_slab, tt, LANES), F32),
                        pltpu.VMEM((n_slab, SUBLANES, LANES), F32)],
        compiler_params=_params(("arbitrary", "arbitrary", "arbitrary")),
        name="rglru",
    )(l, l, l, conv_w, conv_b, wg, br, bi, lam, h0)


COL_GROUP = MXU_DIM
ROW_GROUP = SUBLANES


def _dft_tables(rows, cols, group_w, groups):
    scale = 1.0 / math.sqrt(rows * cols * group_w)
    ang = lambda n: 2.0 * np.pi * ((np.arange(n)[:, None] * np.arange(n)[None, :]) % n) / n
    eye = np.eye
    c_col, s_col = np.kron(eye(COL_GROUP // cols), np.cos(ang(cols))), np.kron(eye(COL_GROUP // cols), np.sin(ang(cols)))
    a1 = np.concatenate([c_col, -s_col], axis=0)
    c_row, s_row = np.kron(np.cos(ang(rows)), eye(ROW_GROUP)) * scale, np.kron(np.sin(ang(rows)), eye(ROW_GROUP)) * scale
    a2 = np.block([[c_row, s_row], [-s_row, c_row]])
    d2 = np.concatenate([np.kron(eye(groups), np.cos(ang(group_w))), np.kron(eye(groups), np.sin(ang(group_w)))], axis=0)
    as_bf16 = lambda m: jnp.asarray(m.astype(np.float32)).astype(BF16)
    return as_bf16(a1), as_bf16(a2), as_bf16(d2)


def _fourier_kernel(f_ref, a1_ref, a2_ref, d2_ref, o_ref, yre_ref, yim_ref, p_ref, q_ref, *, cols, tm):
    n_tok, df = f_ref.shape[1], f_ref.shape[2]
    rows = n_tok // cols
    for t0 in range(0, n_tok, COL_GROUP):
        y = jnp.dot(a1_ref[...], f_ref[0, t0:t0 + COL_GROUP, :], preferred_element_type=F32)
        yre_ref[t0:t0 + COL_GROUP, :] = y[:COL_GROUP]
        yim_ref[t0:t0 + COL_GROUP, :] = y[COL_GROUP:]
    n_grp = rows * ROW_GROUP
    for c0 in range(0, cols, ROW_GROUP):
        slabs = [slice(r * cols + c0, r * cols + c0 + ROW_GROUP) for r in range(rows)]
        y = jnp.concatenate([yre_ref[sl, :] for sl in slabs] + [yim_ref[sl, :] for sl in slabs], axis=0)
        z = jnp.dot(a2_ref[...], y.astype(BF16), preferred_element_type=F32)
        for r, sl in enumerate(slabs):
            p_ref[sl, :] = z[r * ROW_GROUP:(r + 1) * ROW_GROUP]
            q_ref[sl, :] = z[n_grp + r * ROW_GROUP:n_grp + (r + 1) * ROW_GROUP]
    for m in range(0, n_tok, tm):
        o = jnp.dot(p_ref[m:m + tm, :].astype(BF16), d2_ref[:df, :], preferred_element_type=F32)
        o = o + jnp.dot(q_ref[m:m + tm, :].astype(BF16), d2_ref[df:, :], preferred_element_type=F32)
        o_ref[0, m:m + tm, :] = o.astype(o_ref.dtype)


def _fourier(f, a1, a2, d2, cols, tm=512):
    bsz, n_tok, df = f.shape
    return pl.pallas_call(
        functools.partial(_fourier_kernel, cols=cols, tm=tm),
        grid=(bsz,),
        in_specs=[pl.BlockSpec((1, n_tok, df), lambda b: (b, 0, 0)),
                  _resident(a1.shape), _resident(a2.shape), _resident(d2.shape)],
        out_specs=pl.BlockSpec((1, n_tok, df), lambda b: (b, 0, 0)),
        out_shape=jax.ShapeDtypeStruct(f.shape, BF16),
        scratch_shapes=[pltpu.VMEM((n_tok, df), F32) for _ in range(4)],
        compiler_params=_params(("arbitrary",)),
        name="fourier",
    )(f, a1, a2, d2)


def _mid_kernel(x_ref, sh_ref, sc_ref, g_ref, w_ref, cw_ref, cb_ref, wg_ref, br_ref, bi_ref, lam_ref, h0_ref,
                f_ref, l_ref, gg_ref, ga_ref, gb_ref, hf_ref, ext_ref, a_ref, b_ref, ho_ref, carry_ref,
                *, tm, nt, n_tiles, half, chunk, splits):
    s = pl.program_id(0)
    ip = jnp.minimum(s, n_tiles - 1) % nt
    ir = jnp.maximum(s - 1, 0) % nt
    n_slab = a_ref.shape[0]
    (f0, fw), (l0, lw), (g0, gw), (ga0, gaw), (gb0, gbw) = splits

    @pl.when(s == 0)
    def _():
        ext_ref[...] = jnp.zeros(ext_ref.shape, F32)
        carry_ref[...] = jnp.zeros(carry_ref.shape, F32)

    h = (_rms(x_ref[0], g_ref[...]) * (1.0 + sc_ref[0]) + sh_ref[0]).astype(BF16)

    def project(o_ref, start, c, act):
        r = jnp.dot(h, w_ref[:, start + c:start + c + chunk], preferred_element_type=F32)
        o_ref[0, :, c:c + chunk] = (r if act is None else act(r)).astype(o_ref.dtype)

    c1 = _lru_c1(lam_ref[...])
    rows = lax.broadcasted_iota(jnp.int32, (SUBLANES, LANES), 0)
    carries = []
    for j in range(n_slab):
        sl = slice(j * LANES, (j + 1) * LANES)
        h0 = jnp.broadcast_to(h0_ref[0, :, sl], (SUBLANES, LANES))
        carries.append(jnp.where(ir == 0, h0, carry_ref[j]))

    def close_window():
        for j in range(n_slab):
            sl = slice(j * LANES, (j + 1) * LANES)
            ext_ref[j, SUBLANES + tm:2 * SUBLANES + tm, :] = jnp.where(ir == nt - 1, 0.0, l_ref[0, 0:SUBLANES, sl])

    xcs = {}

    def conv_piece(row0, j):
        sl = slice(j * LANES, (j + 1) * LANES)
        xcs[row0, j] = _lru_conv(ext_ref, j, row0, half, cw_ref[:, sl], cb_ref[:, sl])

    def gate_piece(row0, j):
        sl = slice(j * LANES, (j + 1) * LANES)
        _lru_gate_math(a_ref, b_ref, j, xcs.pop((row0, j)), wg_ref[j], br_ref[:, sl], bi_ref[:, sl], c1[:, sl])

    def scan_piece(row0, base):
        carries[:] = [_scan_group(a_ref, b_ref, ho_ref, j, base, row0 + base, carries[j], rows, False)
                      for j in range(n_slab)]

    scan_pieces = []
    for row0 in range(0, tm, half):
        if row0 + half == tm:
            scan_pieces.append(close_window)
        scan_pieces.append(functools.partial(conv_piece, row0, 0))
        for j in range(n_slab):
            if j + 1 < n_slab:
                scan_pieces.append(functools.partial(conv_piece, row0, j + 1))
            scan_pieces.append(functools.partial(gate_piece, row0, j))
        scan_pieces += [functools.partial(scan_piece, row0, base) for base in range(0, half, SCAN_ROWS)]

    proj_pieces = [functools.partial(project, o_ref, start, c, act)
                   for o_ref, start, width, act in ((l_ref, l0, lw, None), (f_ref, f0, fw, None),
                                                    (gg_ref, g0, gw, _gelu_tanh), (ga_ref, ga0, gaw, _sigmoid),
                                                    (gb_ref, gb0, gbw, _sigmoid))
                   for c in range(0, width, chunk)]
    assert half < tm and lw // chunk <= len(proj_pieces) // 2
    for piece in _interleave(scan_pieces, proj_pieces):
        piece()

    for j in range(n_slab):
        sl = slice(j * LANES, (j + 1) * LANES)
        carry_ref[j] = carries[j]
        hf_ref[0, 0, :, sl] = ho_ref[j].astype(hf_ref.dtype)
        ext_ref[j, 0:SUBLANES, :] = jnp.where(ip == 0, 0.0, ext_ref[j, tm:tm + SUBLANES, :])
        ext_ref[j, SUBLANES:SUBLANES + tm, :] = l_ref[0, :, sl]


def _mid(x, mod, k_shift, g, w, splits, conv_w, conv_b, wg, br, bi, lam, h0, tm=512, half=256, chunk=MXU_DIM):
    bsz, seq, d = x.shape
    ch = splits[1][1]
    nt = seq // tm
    n_tiles = bsz * nt
    n_slab = ch // LANES

    def proj_tile(s):
        p = jnp.minimum(s, n_tiles - 1)
        return p // nt, p % nt

    def scan_tile(s):
        r = jnp.maximum(s - 1, 0)
        return r // nt, r % nt

    tok = lambda wd: pl.BlockSpec((1, tm, wd), lambda s: proj_tile(s) + (0,))
    mod_spec = lambda k: pl.BlockSpec((1, 1, d), lambda s: (proj_tile(s)[0], 0, k))
    widths = [wd for _, wd in splits]
    return pl.pallas_call(
        functools.partial(_mid_kernel, tm=tm, nt=nt, n_tiles=n_tiles, half=half, chunk=chunk, splits=splits),
        grid=(n_tiles + 1,),
        in_specs=[tok(d), mod_spec(k_shift), mod_spec(k_shift + 1), _resident((1, d)), _resident(w.shape),
                  _resident(conv_w.shape), _resident((1, ch)), _resident(wg.shape),
                  _resident((1, ch)), _resident((1, ch)), _resident((1, ch)),
                  pl.BlockSpec((1, 1, ch), lambda s: (scan_tile(s)[0], 0, 0))],
        out_specs=[tok(wd) for wd in widths] + [
            pl.BlockSpec((1, 1, tm, ch), lambda s: (0,) + scan_tile(s) + (0,))],
        out_shape=[jax.ShapeDtypeStruct((bsz, seq, wd), dt)
                   for wd, dt in zip(widths, (BF16, F32, BF16, BF16, BF16))] + [
            jax.ShapeDtypeStruct((1, bsz, seq, ch), BF16)],
        scratch_shapes=[pltpu.VMEM((n_slab, tm + 2 * SUBLANES, LANES), F32),
                        pltpu.VMEM((n_slab, half, LANES), F32),
                        pltpu.VMEM((n_slab, half, LANES), F32),
                        pltpu.VMEM((n_slab, tm, LANES), F32),
                        pltpu.VMEM((n_slab, SUBLANES, LANES), F32)],
        compiler_params=_params(("arbitrary",)),
        name="mid",
    )(x, mod, mod, g, w, conv_w, conv_b, wg, br, bi, lam, h0)


def _tail_kernel(l_ref, lp_ref, ln_ref, cw_ref, cb_ref, wg_ref, br_ref, bi_ref, lam_ref, h0_ref,
                 x_ref, hf_ref, g_ref, ga_ref, gb_ref, fo_ref,
                 gate2_ref, sh3_ref, sc3_ref, gate3_ref, gn_ref, gpre_ref, gpost_ref,
                 wfa_ref, wfb_ref, wo_ref, wi2_ref, wo2_ref,
                 o_ref, ext_ref, a_ref, b_ref, hb_ref, carry_ref, act_ref, *, tm, nt, n_tiles, half, chunk):
    s = pl.program_id(0)
    q = jnp.minimum(s, n_tiles - 1)
    cq = nt - 1 - q % nt
    n_slab = a_ref.shape[0]

    @pl.when(s == 0)
    def _():
        hb_ref[...] = jnp.zeros(hb_ref.shape, F32)
        carry_ref[...] = jnp.zeros(carry_ref.shape, F32)

    hb = jnp.concatenate([hb_ref[j] for j in range(n_slab)], axis=1)
    u = ((hf_ref[0, 0].astype(F32) + hb) * g_ref[0].astype(F32)).astype(BF16)
    yb = jnp.dot(u, wfb_ref[...], preferred_element_type=F32)
    ya = jnp.dot(fo_ref[0], wfa_ref[...], preferred_element_type=F32)
    m = ga_ref[0].astype(F32) * ya + gb_ref[0].astype(F32) * yb
    mx = jnp.dot(m.astype(BF16), wo_ref[...], preferred_element_type=F32)
    x2 = x_ref[0] + gate2_ref[0] * _rms(mx, gn_ref[...])

    c1 = _lru_c1(lam_ref[...])
    rows = lax.broadcasted_iota(jnp.int32, (SUBLANES, LANES), 0)
    carries = []
    for j in range(n_slab):
        sl = slice(j * LANES, (j + 1) * LANES)
        _lru_fill_ext(ext_ref, j, lp_ref[0, :, sl], l_ref[0, :, sl], ln_ref[0, :, sl], cq == 0, cq == nt - 1)
        h0 = jnp.broadcast_to(h0_ref[0, :, sl], (SUBLANES, LANES))
        carries.append(jnp.where(cq == nt - 1, h0, carry_ref[j]))

    xcs = {}

    def conv_piece(row0, j):
        sl = slice(j * LANES, (j + 1) * LANES)
        xcs[row0, j] = _lru_conv(ext_ref, j, row0, half, cw_ref[:, sl], cb_ref[:, sl])

    def gate_piece(row0, j):
        sl = slice(j * LANES, (j + 1) * LANES)
        _lru_gate_math(a_ref, b_ref, j, xcs.pop((row0, j)), wg_ref[j], br_ref[:, sl], bi_ref[:, sl], c1[:, sl])

    def scan_piece(row0, base):
        carries[:] = [_scan_group(a_ref, b_ref, hb_ref, j, base, row0 + base, carries[j], rows, True)
                      for j in range(n_slab)]

    pieces = []
    for row0 in range(tm - half, -1, -half):
        pieces.append(functools.partial(conv_piece, row0, 0))
        for j in range(n_slab):
            if j + 1 < n_slab:
                pieces.append(functools.partial(conv_piece, row0, j + 1))
            pieces.append(functools.partial(gate_piece, row0, j))
        pieces += [functools.partial(scan_piece, row0, base) for base in range(half - SCAN_ROWS, -1, -SCAN_ROWS)]

    def between(i, n):
        for piece in pieces[i * len(pieces) // n:(i + 1) * len(pieces) // n]:
            piece()

    _ffn_hidden(_ffn_pre(x2, sh3_ref[0], sc3_ref[0], gpre_ref[...]), wi2_ref, act_ref, chunk, between)
    o_ref[0] = _ffn_post(x2, gate3_ref[0], gpost_ref[...], act_ref, wo2_ref)
    for j in range(n_slab):
        carry_ref[j] = carries[j]


def _tail(l, conv_w, conv_b, wg, br, bi, lam, h0, x, hf, g, ga, gb, fo, mod, gn, gpre, gpost,
          wfa, wfb, wo, wi2, wo2, tm=512, half=256, chunk=MXU_DIM):
    bsz, seq, d = x.shape
    ch = l.shape[2]
    nt = seq // tm
    n_tiles = bsz * nt
    n_slab = ch // LANES
    dff = wo2.shape[0]

    def scan_tile(s):
        q = jnp.minimum(s, n_tiles - 1)
        return q // nt, nt - 1 - q % nt

    def merge_tile(s):
        r = jnp.clip(s - 1, 0, n_tiles - 1)
        return r // nt, nt - 1 - r % nt

    tok = lambda w: pl.BlockSpec((1, tm, w), lambda s: merge_tile(s) + (0,))
    mod_spec = lambda k: pl.BlockSpec((1, 1, d), lambda s: (merge_tile(s)[0], 0, k))
    return pl.pallas_call(
        functools.partial(_tail_kernel, tm=tm, nt=nt, n_tiles=n_tiles, half=half, chunk=chunk),
        grid=(n_tiles + 1,),
        in_specs=_halo_specs(tm, seq, ch, scan_tile) + [
            _resident(conv_w.shape), _resident((1, ch)), _resident(wg.shape),
            _resident((1, ch)), _resident((1, ch)), _resident((1, ch)),
            pl.BlockSpec((1, 1, ch), lambda s: (scan_tile(s)[0], 0, 0)),
            tok(d),
            pl.BlockSpec((1, 1, tm, ch), lambda s: (0,) + merge_tile(s) + (0,)),
            tok(ch), tok(d), tok(d), tok(fo.shape[2]),
            mod_spec(5), mod_spec(6), mod_spec(7), mod_spec(8),
            _resident((1, d)), _resident((1, d)), _resident((1, d)),
            _resident(wfa.shape), _resident(wfb.shape), _resident(wo.shape),
            _resident(wi2.shape), _resident(wo2.shape)],
        out_specs=tok(d),
        out_shape=jax.ShapeDtypeStruct(x.shape, F32),
        scratch_shapes=[pltpu.VMEM((n_slab, tm + 2 * SUBLANES, LANES), F32),
                        pltpu.VMEM((n_slab, half, LANES), F32),
                        pltpu.VMEM((n_slab, half, LANES), F32),
                        pltpu.VMEM((n_slab, tm, LANES), F32),
                        pltpu.VMEM((n_slab, SUBLANES, LANES), F32),
                        pltpu.VMEM((tm, dff), BF16)],
        compiler_params=_params(("arbitrary",), TAIL_VMEM_LIMIT_BYTES),
        name="tail",
    )(l, l, l, conv_w, conv_b, wg, br, bi, lam, h0,
      x, hf, g, ga, gb, fo, mod, mod, mod, mod, gn, gpre, gpost, wfa, wfb, wo, wi2, wo2)


def kernel(x, c, ctx, c_ctx, w_ada, b_ada, norm_g, w_ffn1_in, w_ffn1_out, w_ffn2_in, w_ffn2_out,
           w_in, conv_w, conv_b, w_r, b_r, w_i, b_i, lam, w_fa, w_fb, w_out):
    depth = w_ada.shape[0]
    assert depth == 1, "single-layer problem: the context stream is only needed up to the mixer scans"
    bsz, seq, d = x.shape
    d_lru = w_fb.shape[1]
    d_f = w_fa.shape[1]
    lyr = 0

    pad = (-(bsz + 1)) % SUBLANES
    cc = jnp.concatenate([c, c_ctx[None, :], jnp.zeros((pad, d), F32)], axis=0)
    mod = _ada(cc, w_ada[lyr], b_ada[lyr][None, :])
    mod = mod.reshape(mod.shape[0], 1, N_MOD * d)
    lat_row = lambda b: b
    ctx_row = lambda b: bsz
    g = norm_g[lyr][:, None, :]

    wi1, wo1 = w_ffn1_in[lyr].astype(BF16), w_ffn1_out[lyr].astype(BF16)

    x1, wi2, wo2, w_in_b, wfa_b, wfb_b, wout_b = _ffn(
        x, mod, lat_row, 0, g[0], g[1], wi1, wo1,
        casts=(w_ffn2_in[lyr], w_ffn2_out[lyr], w_in[lyr], w_fa[lyr], w_fb[lyr], w_out[lyr]), tm=1024)
    ctx_flat = ctx.reshape(1, bsz * ctx.shape[1], d)
    c1 = _ffn(ctx_flat, mod, ctx_row, 0, g[0], g[1], wi1, wo1)

    (lc,) = _inproj(c1, mod, ctx_row, 3, g[2], w_in_b[:, d_f:d_f + d_lru], ((0, d_lru),), (F32,), (None,))
    lc = lc.reshape(bsz, ctx.shape[1], d_lru)

    wgate = (0.5 * jnp.concatenate([w_r[lyr], w_i[lyr]], axis=-1)).astype(BF16)
    br, bi, lm = 0.5 * b_r[lyr][:, None, :], 0.5 * b_i[lyr][:, None, :], lam[lyr][:, None, :]
    cw, cb = conv_w[lyr], conv_b[lyr][None, :]
    h_ctx = _rglru(lc, cw, cb, wgate, br, bi, lm, jnp.zeros((2, bsz, 1, d_lru), F32), 2, tt=ctx.shape[1])
    h0f = h_ctx[0, :, -1][:, None, :]
    h0b = h_ctx[1, :, 0][:, None, :]

    splits = ((0, d_f), (d_f, d_lru), (d_f + d_lru, d_lru), (d_f + 2 * d_lru, d), (d_f + 2 * d_lru + d, d))
    fx, lx, gx, gax, gbx, hf = _mid(x1, mod, 3, g[2], w_in_b, splits, cw, cb, wgate[0], br[0], bi[0], lm[0], h0f)

    a1, a2, d2 = _dft_tables(seq // GRID_W, GRID_W, d_f // FOURIER_GROUPS, FOURIER_GROUPS)
    fo = _fourier(fx, a1, a2, d2, GRID_W)

    return _tail(lx, cw, cb, wgate[1], br[1], bi[1], lm[1], h0b, x1, hf, gx, gax, gbx, fo, mod,
                 g[3], g[4], g[5], wfa_b, wfb_b, wout_b, wi2, wo2)
```

```python
import functools
import math

import numpy as np
import jax
import jax.numpy as jnp
from jax import lax
from jax.experimental import pallas as pl
from jax.experimental.pallas import tpu as pltpu

LANES = 128
SUBLANES = 8
MXU_DIM = 256
VMEM_LIMIT_BYTES = 56 * 1024 * 1024
TAIL_VMEM_LIMIT_BYTES = 62 * 1024 * 1024

GRID_W = 64
FOURIER_GROUPS = 4
CONV_W = 4
CONV_LEFT = (CONV_W - 1) // 2
GATE_C = 8.0
N_MOD = 9
EPS = 1e-6

BF16 = jnp.bfloat16
F32 = jnp.float32

SEG_LEN = 4
SCAN_ROWS = SEG_LEN * SUBLANES


def _sigmoid(x):
    return 0.5 * jnp.tanh(0.5 * x) + 0.5


def _silu(x):
    return x * _sigmoid(x)


def _gelu_tanh(x):
    c = math.sqrt(2.0 / math.pi)
    return 0.5 * x * (1.0 + jnp.tanh(c * (x + 0.044715 * (x * x * x))))


def _rms(x, g):
    return x * lax.rsqrt(jnp.mean(x * x, axis=-1, keepdims=True) + EPS) * g


def _resident(shape):
    nd = len(shape)
    return pl.BlockSpec(shape, lambda *_: (0,) * nd, pipeline_mode=pl.Buffered(1))


def _params(semantics, vmem=VMEM_LIMIT_BYTES):
    return pltpu.CompilerParams(dimension_semantics=semantics, vmem_limit_bytes=vmem)


def _interleave(major, minor):
    merged, done = [], 0
    for i, piece in enumerate(major):
        while done < len(minor) and done * len(major) <= i * len(minor):
            merged.append(minor[done])
            done += 1
        merged.append(piece)
    return merged + list(minor[done:])


def _ada_kernel(c_ref, w_ref, b_ref, o_ref):
    s = _silu(c_ref[...]).astype(BF16)
    o_ref[...] = jnp.dot(s, w_ref[...].astype(BF16), preferred_element_type=F32) + b_ref[...]


def _ada(cc, w, b, tn=1536):
    m, d = cc.shape
    n = w.shape[1]
    return pl.pallas_call(
        _ada_kernel,
        grid=(n // tn,),
        in_specs=[pl.BlockSpec((m, d), lambda j: (0, 0)),
                  pl.BlockSpec((d, tn), lambda j: (0, j)),
                  pl.BlockSpec((1, tn), lambda j: (0, j))],
        out_specs=pl.BlockSpec((m, tn), lambda j: (0, j)),
        out_shape=jax.ShapeDtypeStruct((m, n), F32),
        compiler_params=_params(("arbitrary",)),
        name="adaln",
    )(cc, w, b)


def _ffn_pre(x, sh, sc, gpre):
    return (_rms(x, gpre) * (1.0 + sc) + sh).astype(BF16)


def _ffn_hidden(h, wi_ref, act_ref, chunk, between=None):
    dff = act_ref.shape[1]
    for i, c in enumerate(range(0, dff, chunk)):
        gate = jnp.dot(h[...], wi_ref[:, c:c + chunk], preferred_element_type=F32)
        up = jnp.dot(h[...], wi_ref[:, dff + c:dff + c + chunk], preferred_element_type=F32)
        act_ref[:, c:c + chunk] = (_silu(gate) * up).astype(BF16)
        if between is not None:
            between(i, dff // chunk)


def _ffn_post(x, ga, gpost, act_ref, wo_ref):
    y = jnp.dot(act_ref[...], wo_ref[...], preferred_element_type=F32)
    return x + (0.5 * ga) * _rms(y, gpost)


def _ffn_kernel(x_ref, sh_ref, sc_ref, ga_ref, gpre_ref, gpost_ref, wi_ref, wo_ref, *rest, chunk, n_cast):
    cast_in, o_ref, cast_out, act_ref = rest[:n_cast], rest[n_cast], rest[n_cast + 1:-1], rest[-1]
    x = x_ref[0]
    _ffn_hidden(_ffn_pre(x, sh_ref[0], sc_ref[0], gpre_ref[...]), wi_ref, act_ref, chunk)
    o_ref[0] = _ffn_post(x, ga_ref[0], gpost_ref[...], act_ref, wo_ref)
    for src, dst in zip(cast_in, cast_out):
        dst[...] = src[...].astype(dst.dtype)


def _ffn(x, mod, mod_row, k_shift, g_pre, g_post, wi, wo, casts=(), tm=512, chunk=MXU_DIM):
    bsz, seq, d = x.shape
    dff = wo.shape[0]
    nt = seq // tm
    n_steps = bsz * nt
    mod_spec = lambda k: pl.BlockSpec((1, 1, d), lambda b, i: (mod_row(b), 0, k))

    def cast_spec(w):
        rows = w.shape[0]
        n_blocks = max(n for n in range(1, n_steps + 1)
                       if n_steps % n == 0 and rows % n == 0 and (rows // n) % (2 * SUBLANES) == 0)
        rep = n_steps // n_blocks
        return pl.BlockSpec((rows // n_blocks, w.shape[1]), lambda b, i: ((b * nt + i) // rep, 0))

    cast_specs = lambda: [cast_spec(w) for w in casts]
    out = pl.pallas_call(
        functools.partial(_ffn_kernel, chunk=chunk, n_cast=len(casts)),
        grid=(bsz, nt),
        in_specs=[pl.BlockSpec((1, tm, d), lambda b, i: (b, i, 0)),
                  mod_spec(k_shift), mod_spec(k_shift + 1), mod_spec(k_shift + 2),
                  _resident((1, d)), _resident((1, d)),
                  _resident(wi.shape), _resident(wo.shape)] + cast_specs(),
        out_specs=[pl.BlockSpec((1, tm, d), lambda b, i: (b, i, 0))] + cast_specs(),
        out_shape=[jax.ShapeDtypeStruct(x.shape, F32)] + [jax.ShapeDtypeStruct(w.shape, BF16) for w in casts],
        scratch_shapes=[pltpu.VMEM((tm, dff), BF16)],
        compiler_params=_params(("arbitrary", "arbitrary")),
        name="ffn",
    )(x, mod, mod, mod, g_pre, g_post, wi, wo, *casts)
    return out if casts else out[0]


def _lru_c1(lam):
    e = jnp.exp(-jnp.abs(lam))
    u = 1.0 + e
    log1p_e = jnp.where(u == 1.0, e, jnp.log(u) * (e / (u - 1.0)))
    return (-0.5 * GATE_C * math.log2(math.e)) * (jnp.maximum(-lam, 0.0) + log1p_e)


def _lru_fill_ext(ext_ref, j, prev, main, nxt, at_start, at_end):
    tt = main.shape[0]
    ext_ref[j, 0:SUBLANES, :] = jnp.where(at_start, 0.0, prev)
    ext_ref[j, SUBLANES:SUBLANES + tt, :] = main
    ext_ref[j, SUBLANES + tt:2 * SUBLANES + tt, :] = jnp.where(at_end, 0.0, nxt)


def _lru_conv(ext_ref, j, row0, nrows, cw, cb):
    xc = cb
    for k in range(CONV_W):
        off = SUBLANES - CONV_LEFT + k + row0
        tap = jnp.concatenate(
            [ext_ref[j, pl.ds(off + m + kk, SUBLANES, stride=SEG_LEN), :]
             for m in range(0, nrows, SCAN_ROWS) for kk in range(SEG_LEN)], axis=0)
        xc = xc + tap * cw[k:k + 1, :]
    return xc


def _lru_gate_math(a_ref, b_ref, j, xc, wg, br, bi, c1):
    z = jnp.dot(xc.astype(BF16), wg, preferred_element_type=F32)
    t_r = jnp.tanh(z[:, :LANES] + br)
    t_i = jnp.tanh(z[:, LANES:] + bi)
    a = jnp.exp2(c1 * t_r + c1)
    y = 1.0 - a * a
    root = jnp.where(y > 0.0, y * lax.rsqrt(y), 0.0)
    hx = 0.5 * xc
    a_ref[j] = a
    b_ref[j] = root * (t_i * hx + hx)


def _shift_rows(x, k, fill, rows):
    return jnp.where(rows >= k, pltpu.roll(x, k, axis=0), fill)


def _shift_rows_up(x, k, fill, rows):
    return jnp.where(rows < SUBLANES - k, pltpu.roll(x, SUBLANES - k, axis=0), fill)


def _scan_group(a_ref, b_ref, o_ref, j, base, out_base, carry, rows, reverse):
    ks = range(SEG_LEN - 1, -1, -1) if reverse else range(SEG_LEN)
    shift = _shift_rows_up if reverse else _shift_rows
    a = {k: a_ref[j, pl.ds(base + k * SUBLANES, SUBLANES), :] for k in ks}
    b = {k: b_ref[j, pl.ds(base + k * SUBLANES, SUBLANES), :] for k in ks}
    h, p = {}, {}
    prev = None
    for k in ks:
        if prev is None:
            h[k], p[k] = b[k], a[k]
        else:
            h[k] = a[k] * h[prev] + b[k]
            p[k] = a[k] * p[prev]
        prev = k
    hc, pc = h[prev], p[prev]
    for dd in (1, 2, 4):
        hc = pc * shift(hc, dd, 0.0, rows) + hc
        pc = pc * shift(pc, dd, 1.0, rows)
    g = hc + pc * carry
    if o_ref is not None:
        cin = shift(g, 1, carry, rows)
        for k in ks:
            o_ref[j, pl.ds(out_base + k, SUBLANES, stride=SEG_LEN), :] = h[k] + p[k] * cin
    last = 0 if reverse else SUBLANES - 1
    return jnp.broadcast_to(g[last:last + 1, :], (SUBLANES, LANES))


def _ctx_kernel(c_ref, sh_ref, sc_ref, g_ref, w_ref, cw_ref, cb_ref, wg_ref, br_ref, bi_ref, lam_ref,
                hf_ref, hb_ref, ext_ref, a_ref, b_ref, *, l0):
    tt = c_ref.shape[1]
    n_slab, n_dir = a_ref.shape[1], a_ref.shape[0]
    h = (_rms(c_ref[0], g_ref[...]) * (1.0 + sc_ref[0]) + sh_ref[0]).astype(BF16)
    c1 = [_lru_c1(lam_ref[d]) for d in range(n_dir)]
    halo = jnp.zeros((SUBLANES, LANES), F32)
    for j0 in range(0, n_slab, MXU_DIM // LANES):
        cols = slice(l0 + j0 * LANES, l0 + j0 * LANES + MXU_DIM)
        l_pair = jnp.dot(h, w_ref[:, cols], preferred_element_type=F32)
        for j in range(j0, j0 + MXU_DIM // LANES):
            sl = slice(j * LANES, (j + 1) * LANES)
            _lru_fill_ext(ext_ref, j, halo, l_pair[:, (j - j0) * LANES:(j - j0 + 1) * LANES], halo, False, False)
            xc = _lru_conv(ext_ref, j, 0, tt, cw_ref[:, sl], cb_ref[:, sl])
            for d in range(n_dir):
                _lru_gate_math(a_ref.at[d], b_ref.at[d], j, xc, wg_ref[d, j], br_ref[d, :, sl], bi_ref[d, :, sl],
                               c1[d][:, sl])
    rows = lax.broadcasted_iota(jnp.int32, (SUBLANES, LANES), 0)
    for d, (o_ref, reverse) in enumerate(((hf_ref, False), (hb_ref, True))):
        carries = [jnp.zeros((SUBLANES, LANES), F32)] * n_slab
        bases = range(tt - SCAN_ROWS, -1, -SCAN_ROWS) if reverse else range(0, tt, SCAN_ROWS)
        for base in bases:
            carries = [_scan_group(a_ref.at[d], b_ref.at[d], None, j, base, base, carries[j], rows, reverse)
                       for j in range(n_slab)]
        for j in range(n_slab):
            o_ref[0, :, j * LANES:(j + 1) * LANES] = carries[j][0:1, :]


def _ctx(c1, mod, mod_row, k_shift, g, w, l0, conv_w, conv_b, wg, br, bi, lam):
    bsz, tt, d = c1.shape
    n_dir, n_slab = wg.shape[0], wg.shape[1]
    ch = n_slab * LANES
    mod_spec = lambda k: pl.BlockSpec((1, 1, d), lambda b: (mod_row, 0, k))
    end_state = lambda: pl.BlockSpec((1, 1, ch), lambda b: (b, 0, 0))
    return pl.pallas_call(
        functools.partial(_ctx_kernel, l0=l0),
        grid=(bsz,),
        in_specs=[pl.BlockSpec((1, tt, d), lambda b: (b, 0, 0)), mod_spec(k_shift), mod_spec(k_shift + 1),
                  _resident((1, d)), _resident(w.shape), _resident(conv_w.shape), _resident((1, ch)),
                  _resident(wg.shape), _resident(br.shape), _resident(bi.shape), _resident(lam.shape)],
        out_specs=[end_state(), end_state()],
        out_shape=[jax.ShapeDtypeStruct((bsz, 1, ch), F32)] * 2,
        scratch_shapes=[pltpu.VMEM((n_slab, tt + 2 * SUBLANES, LANES), F32),
                        pltpu.VMEM((n_dir, n_slab, tt, LANES), F32),
                        pltpu.VMEM((n_dir, n_slab, tt, LANES), F32)],
        compiler_params=_params(("arbitrary",)),
        name="ctx_scan",
    )(c1, mod, mod, g, w, conv_w, conv_b, wg, br, bi, lam)


def _halo_specs(tt, t, ch, idx):
    hb = tt // SUBLANES
    last = t // SUBLANES - 1

    def prev_map(*g):
        b, tc = idx(*g)
        return b, jnp.maximum(tc * hb - 1, 0), 0

    def next_map(*g):
        b, tc = idx(*g)
        return b, jnp.minimum((tc + 1) * hb, last), 0

    return [pl.BlockSpec((1, tt, ch), lambda *g: idx(*g) + (0,)),
            pl.BlockSpec((1, SUBLANES, ch), prev_map),
            pl.BlockSpec((1, SUBLANES, ch), next_map)]


COL_GROUP = MXU_DIM
ROW_GROUP = SUBLANES


def _dft_tables(rows, cols, group_w, groups):
    scale = 1.0 / math.sqrt(rows * cols * group_w)
    ang = lambda n: 2.0 * np.pi * ((np.arange(n)[:, None] * np.arange(n)[None, :]) % n) / n
    eye = np.eye
    c_col, s_col = np.kron(eye(COL_GROUP // cols), np.cos(ang(cols))), np.kron(eye(COL_GROUP // cols), np.sin(ang(cols)))
    a1 = np.concatenate([c_col, -s_col], axis=0)
    c_row, s_row = np.kron(np.cos(ang(rows)), eye(ROW_GROUP)) * scale, np.kron(np.sin(ang(rows)), eye(ROW_GROUP)) * scale
    a2 = np.block([[c_row, s_row], [-s_row, c_row]])
    d2 = np.concatenate([np.kron(eye(groups), np.cos(ang(group_w))), np.kron(eye(groups), np.sin(ang(group_w)))], axis=0)
    as_bf16 = lambda m: jnp.asarray(m.astype(np.float32)).astype(BF16)
    return as_bf16(a1), as_bf16(a2), as_bf16(d2)


def _fourier_kernel(f_ref, a1_ref, a2_ref, d2_ref, o_ref, yre_ref, yim_ref, p_ref, q_ref, *, cols, tm):
    n_tok, df = f_ref.shape[1], f_ref.shape[2]
    rows = n_tok // cols
    for t0 in range(0, n_tok, COL_GROUP):
        y = jnp.dot(a1_ref[...], f_ref[0, t0:t0 + COL_GROUP, :], preferred_element_type=F32)
        yre_ref[t0:t0 + COL_GROUP, :] = y[:COL_GROUP]
        yim_ref[t0:t0 + COL_GROUP, :] = y[COL_GROUP:]
    n_grp = rows * ROW_GROUP
    for c0 in range(0, cols, ROW_GROUP):
        slabs = [slice(r * cols + c0, r * cols + c0 + ROW_GROUP) for r in range(rows)]
        y = jnp.concatenate([yre_ref[sl, :] for sl in slabs] + [yim_ref[sl, :] for sl in slabs], axis=0)
        z = jnp.dot(a2_ref[...], y.astype(BF16), preferred_element_type=F32)
        for r, sl in enumerate(slabs):
            p_ref[sl, :] = z[r * ROW_GROUP:(r + 1) * ROW_GROUP]
            q_ref[sl, :] = z[n_grp + r * ROW_GROUP:n_grp + (r + 1) * ROW_GROUP]
    for m in range(0, n_tok, tm):
        o = jnp.dot(p_ref[m:m + tm, :].astype(BF16), d2_ref[:df, :], preferred_element_type=F32)
        o = o + jnp.dot(q_ref[m:m + tm, :].astype(BF16), d2_ref[df:, :], preferred_element_type=F32)
        o_ref[0, m:m + tm, :] = o.astype(o_ref.dtype)


def _fourier(f, a1, a2, d2, cols, tm=512):
    bsz, n_tok, df = f.shape
    return pl.pallas_call(
        functools.partial(_fourier_kernel, cols=cols, tm=tm),
        grid=(bsz,),
        in_specs=[pl.BlockSpec((1, n_tok, df), lambda b: (b, 0, 0)),
                  _resident(a1.shape), _resident(a2.shape), _resident(d2.shape)],
        out_specs=pl.BlockSpec((1, n_tok, df), lambda b: (b, 0, 0)),
        out_shape=jax.ShapeDtypeStruct(f.shape, BF16),
        scratch_shapes=[pltpu.VMEM((n_tok, df), F32) for _ in range(4)],
        compiler_params=_params(("arbitrary",)),
        name="fourier",
    )(f, a1, a2, d2)


def _mid_kernel(x_ref, sh_ref, sc_ref, g_ref, w_ref, cw_ref, cb_ref, wg_ref, br_ref, bi_ref, lam_ref, h0_ref,
                f_ref, l_ref, gg_ref, ga_ref, gb_ref, hf_ref, ext_ref, a_ref, b_ref, ho_ref, carry_ref,
                *, tm, nt, n_tiles, half, chunk, splits):
    s = pl.program_id(0)
    ip = jnp.minimum(s, n_tiles - 1) % nt
    ir = jnp.maximum(s - 1, 0) % nt
    n_slab = a_ref.shape[0]
    (f0, fw), (l0, lw), (g0, gw), (ga0, gaw), (gb0, gbw) = splits

    @pl.when(s == 0)
    def _():
        ext_ref[...] = jnp.zeros(ext_ref.shape, F32)
        carry_ref[...] = jnp.zeros(carry_ref.shape, F32)

    h = (_rms(x_ref[0], g_ref[...]) * (1.0 + sc_ref[0]) + sh_ref[0]).astype(BF16)

    def project(o_ref, start, c, act):
        r = jnp.dot(h, w_ref[:, start + c:start + c + chunk], preferred_element_type=F32)
        o_ref[0, :, c:c + chunk] = (r if act is None else act(r)).astype(o_ref.dtype)

    c1 = _lru_c1(lam_ref[...])
    rows = lax.broadcasted_iota(jnp.int32, (SUBLANES, LANES), 0)
    carries = []
    for j in range(n_slab):
        sl = slice(j * LANES, (j + 1) * LANES)
        h0 = jnp.broadcast_to(h0_ref[0, :, sl], (SUBLANES, LANES))
        carries.append(jnp.where(ir == 0, h0, carry_ref[j]))

    def close_window():
        for j in range(n_slab):
            sl = slice(j * LANES, (j + 1) * LANES)
            ext_ref[j, SUBLANES + tm:2 * SUBLANES + tm, :] = jnp.where(ir == nt - 1, 0.0, l_ref[0, 0:SUBLANES, sl])

    xcs = {}

    def conv_piece(row0, j):
        sl = slice(j * LANES, (j + 1) * LANES)
        xcs[row0, j] = _lru_conv(ext_ref, j, row0, half, cw_ref[:, sl], cb_ref[:, sl])

    def gate_piece(row0, j):
        sl = slice(j * LANES, (j + 1) * LANES)
        _lru_gate_math(a_ref, b_ref, j, xcs.pop((row0, j)), wg_ref[j], br_ref[:, sl], bi_ref[:, sl], c1[:, sl])

    def scan_piece(row0, base):
        carries[:] = [_scan_group(a_ref, b_ref, ho_ref, j, base, row0 + base, carries[j], rows, False)
                      for j in range(n_slab)]

    scan_pieces = []
    for row0 in range(0, tm, half):
        if row0 + half == tm:
            scan_pieces.append(close_window)
        scan_pieces.append(functools.partial(conv_piece, row0, 0))
        for j in range(n_slab):
            if j + 1 < n_slab:
                scan_pieces.append(functools.partial(conv_piece, row0, j + 1))
            scan_pieces.append(functools.partial(gate_piece, row0, j))
        scan_pieces += [functools.partial(scan_piece, row0, base) for base in range(0, half, SCAN_ROWS)]

    proj_pieces = [functools.partial(project, o_ref, start, c, act)
                   for o_ref, start, width, act in ((l_ref, l0, lw, None), (f_ref, f0, fw, None),
                                                    (gg_ref, g0, gw, _gelu_tanh), (ga_ref, ga0, gaw, _sigmoid),
                                                    (gb_ref, gb0, gbw, _sigmoid))
                   for c in range(0, width, chunk)]
    assert half < tm and lw // chunk <= len(proj_pieces) // 2
    for piece in _interleave(scan_pieces, proj_pieces):
        piece()

    for j in range(n_slab):
        sl = slice(j * LANES, (j + 1) * LANES)
        carry_ref[j] = carries[j]
        hf_ref[0, 0, :, sl] = ho_ref[j].astype(hf_ref.dtype)
        ext_ref[j, 0:SUBLANES, :] = jnp.where(ip == 0, 0.0, ext_ref[j, tm:tm + SUBLANES, :])
        ext_ref[j, SUBLANES:SUBLANES + tm, :] = l_ref[0, :, sl]


def _mid(x, mod, k_shift, g, w, splits, conv_w, conv_b, wg, br, bi, lam, h0, tm=512, half=128, chunk=MXU_DIM):
    bsz, seq, d = x.shape
    ch = splits[1][1]
    nt = seq // tm
    n_tiles = bsz * nt
    n_slab = ch // LANES

    def proj_tile(s):
        p = jnp.minimum(s, n_tiles - 1)
        return p // nt, p % nt

    def scan_tile(s):
        r = jnp.maximum(s - 1, 0)
        return r // nt, r % nt

    tok = lambda wd: pl.BlockSpec((1, tm, wd), lambda s: proj_tile(s) + (0,))
    mod_spec = lambda k: pl.BlockSpec((1, 1, d), lambda s: (proj_tile(s)[0], 0, k))
    widths = [wd for _, wd in splits]
    return pl.pallas_call(
        functools.partial(_mid_kernel, tm=tm, nt=nt, n_tiles=n_tiles, half=half, chunk=chunk, splits=splits),
        grid=(n_tiles + 1,),
        in_specs=[tok(d), mod_spec(k_shift), mod_spec(k_shift + 1), _resident((1, d)), _resident(w.shape),
                  _resident(conv_w.shape), _resident((1, ch)), _resident(wg.shape),
                  _resident((1, ch)), _resident((1, ch)), _resident((1, ch)),
                  pl.BlockSpec((1, 1, ch), lambda s: (scan_tile(s)[0], 0, 0))],
        out_specs=[tok(wd) for wd in widths] + [
            pl.BlockSpec((1, 1, tm, ch), lambda s: (0,) + scan_tile(s) + (0,))],
        out_shape=[jax.ShapeDtypeStruct((bsz, seq, wd), dt)
                   for wd, dt in zip(widths, (BF16, F32, BF16, BF16, BF16))] + [
            jax.ShapeDtypeStruct((1, bsz, seq, ch), BF16)],
        scratch_shapes=[pltpu.VMEM((n_slab, tm + 2 * SUBLANES, LANES), F32),
                        pltpu.VMEM((n_slab, half, LANES), F32),
                        pltpu.VMEM((n_slab, half, LANES), F32),
                        pltpu.VMEM((n_slab, tm, LANES), F32),
                        pltpu.VMEM((n_slab, SUBLANES, LANES), F32)],
        compiler_params=_params(("arbitrary",)),
        name="mid",
    )(x, mod, mod, g, w, conv_w, conv_b, wg, br, bi, lam, h0)


def _tail_kernel(l_ref, lp_ref, ln_ref, cw_ref, cb_ref, wg_ref, br_ref, bi_ref, lam_ref, h0_ref,
                 x_ref, hf_ref, g_ref, ga_ref, gb_ref, fo_ref,
                 gate2_ref, sh3_ref, sc3_ref, gate3_ref, gn_ref, gpre_ref, gpost_ref,
                 wfa_ref, wfb_ref, wo_ref, wi2_ref, wo2_ref,
                 o_ref, ext_ref, a_ref, b_ref, hb_ref, carry_ref, act_ref,
                 *, tm, nt, n_tiles, half, chunk, merge_split):
    s = pl.program_id(0)
    q = jnp.minimum(s, n_tiles - 1)
    cq = nt - 1 - q % nt
    n_slab = a_ref.shape[0]

    @pl.when(s == 0)
    def _():
        hb_ref[...] = jnp.zeros(hb_ref.shape, F32)
        carry_ref[...] = jnp.zeros(carry_ref.shape, F32)

    blocks = [slice(r0, r0 + tm // merge_split) for r0 in range(0, tm, tm // merge_split)]
    yas = [jnp.dot(fo_ref[0, rs, :], wfa_ref[...], preferred_element_type=F32) for rs in blocks]
    ybs = []
    for rs in blocks:
        hb = jnp.concatenate([hb_ref[j, rs, :] for j in range(n_slab)], axis=1)
        u = ((hf_ref[0, 0, rs, :].astype(F32) + hb) * g_ref[0, rs, :].astype(F32)).astype(BF16)
        ybs.append(jnp.dot(u, wfb_ref[...], preferred_element_type=F32))
    mxs = []
    for rs, ya, yb in zip(blocks, yas, ybs):
        m = ga_ref[0, rs, :].astype(F32) * ya + gb_ref[0, rs, :].astype(F32) * yb
        mxs.append(jnp.dot(m.astype(BF16), wo_ref[...], preferred_element_type=F32))
    x2s = [x_ref[0, rs, :] + gate2_ref[0] * _rms(mx, gn_ref[...]) for rs, mx in zip(blocks, mxs)]
    hs = [_ffn_pre(x2, sh3_ref[0], sc3_ref[0], gpre_ref[...]) for x2 in x2s]
    x2 = jnp.concatenate(x2s, axis=0)
    h2 = jnp.concatenate(hs, axis=0)

    c1 = _lru_c1(lam_ref[...])
    rows = lax.broadcasted_iota(jnp.int32, (SUBLANES, LANES), 0)
    carries = []
    for j in range(n_slab):
        sl = slice(j * LANES, (j + 1) * LANES)
        _lru_fill_ext(ext_ref, j, lp_ref[0, :, sl], l_ref[0, :, sl], ln_ref[0, :, sl], cq == 0, cq == nt - 1)
        h0 = jnp.broadcast_to(h0_ref[0, :, sl], (SUBLANES, LANES))
        carries.append(jnp.where(cq == nt - 1, h0, carry_ref[j]))

    xcs = {}

    def conv_piece(row0, j):
        sl = slice(j * LANES, (j + 1) * LANES)
        xcs[row0, j] = _lru_conv(ext_ref, j, row0, half, cw_ref[:, sl], cb_ref[:, sl])

    def gate_piece(row0, j):
        sl = slice(j * LANES, (j + 1) * LANES)
        _lru_gate_math(a_ref, b_ref, j, xcs.pop((row0, j)), wg_ref[j], br_ref[:, sl], bi_ref[:, sl], c1[:, sl])

    def scan_piece(row0, base):
        carries[:] = [_scan_group(a_ref, b_ref, hb_ref, j, base, row0 + base, carries[j], rows, True)
                      for j in range(n_slab)]

    pieces = []
    for row0 in range(tm - half, -1, -half):
        pieces.append(functools.partial(conv_piece, row0, 0))
        for j in range(n_slab):
            if j + 1 < n_slab:
                pieces.append(functools.partial(conv_piece, row0, j + 1))
            pieces.append(functools.partial(gate_piece, row0, j))
        pieces += [functools.partial(scan_piece, row0, base) for base in range(half - SCAN_ROWS, -1, -SCAN_ROWS)]

    def between(i, n):
        for piece in pieces[i * len(pieces) // n:(i + 1) * len(pieces) // n]:
            piece()

    _ffn_hidden(h2, wi2_ref, act_ref, chunk, between)
    o_ref[0] = _ffn_post(x2, gate3_ref[0], gpost_ref[...], act_ref, wo2_ref)
    for j in range(n_slab):
        carry_ref[j] = carries[j]


def _tail(l, conv_w, conv_b, wg, br, bi, lam, h0, x, hf, g, ga, gb, fo, mod, gn, gpre, gpost,
          wfa, wfb, wo, wi2, wo2, tm=512, half=128, chunk=MXU_DIM, merge_split=2):
    bsz, seq, d = x.shape
    ch = l.shape[2]
    nt = seq // tm
    n_tiles = bsz * nt
    n_slab = ch // LANES
    dff = wo2.shape[0]

    def scan_tile(s):
        q = jnp.minimum(s, n_tiles - 1)
        return q // nt, nt - 1 - q % nt

    def merge_tile(s):
        r = jnp.clip(s - 1, 0, n_tiles - 1)
        return r // nt, nt - 1 - r % nt

    tok = lambda w: pl.BlockSpec((1, tm, w), lambda s: merge_tile(s) + (0,))
    mod_spec = lambda k: pl.BlockSpec((1, 1, d), lambda s: (merge_tile(s)[0], 0, k))
    return pl.pallas_call(
        functools.partial(_tail_kernel, tm=tm, nt=nt, n_tiles=n_tiles, half=half, chunk=chunk,
                          merge_split=merge_split),
        grid=(n_tiles + 1,),
        in_specs=_halo_specs(tm, seq, ch, scan_tile) + [
            _resident(conv_w.shape), _resident((1, ch)), _resident(wg.shape),
            _resident((1, ch)), _resident((1, ch)), _resident((1, ch)),
            pl.BlockSpec((1, 1, ch), lambda s: (scan_tile(s)[0], 0, 0)),
            tok(d),
            pl.BlockSpec((1, 1, tm, ch), lambda s: (0,) + merge_tile(s) + (0,)),
            tok(ch), tok(d), tok(d), tok(fo.shape[2]),
            mod_spec(5), mod_spec(6), mod_spec(7), mod_spec(8),
            _resident((1, d)), _resident((1, d)), _resident((1, d)),
            _resident(wfa.shape), _resident(wfb.shape), _resident(wo.shape),
            _resident(wi2.shape), _resident(wo2.shape)],
        out_specs=tok(d),
        out_shape=jax.ShapeDtypeStruct(x.shape, F32),
        scratch_shapes=[pltpu.VMEM((n_slab, tm + 2 * SUBLANES, LANES), F32),
                        pltpu.VMEM((n_slab, half, LANES), F32),
                        pltpu.VMEM((n_slab, half, LANES), F32),
                        pltpu.VMEM((n_slab, tm, LANES), F32),
                        pltpu.VMEM((n_slab, SUBLANES, LANES), F32),
                        pltpu.VMEM((tm, dff), BF16)],
        compiler_params=_params(("arbitrary",), TAIL_VMEM_LIMIT_BYTES),
        name="tail",
    )(l, l, l, conv_w, conv_b, wg, br, bi, lam, h0,
      x, hf, g, ga, gb, fo, mod, mod, mod, mod, gn, gpre, gpost, wfa, wfb, wo, wi2, wo2)


def kernel(x, c, ctx, c_ctx, w_ada, b_ada, norm_g, w_ffn1_in, w_ffn1_out, w_ffn2_in, w_ffn2_out,
           w_in, conv_w, conv_b, w_r, b_r, w_i, b_i, lam, w_fa, w_fb, w_out):
    depth = w_ada.shape[0]
    assert depth == 1, "single-layer problem: the context stream is only needed up to the mixer scans"
    bsz, seq, d = x.shape
    d_lru = w_fb.shape[1]
    d_f = w_fa.shape[1]
    lyr = 0

    pad = (-(bsz + 1)) % SUBLANES
    cc = jnp.concatenate([c, c_ctx[None, :], jnp.zeros((pad, d), F32)], axis=0)
    mod = _ada(cc, w_ada[lyr], b_ada[lyr][None, :])
    mod = mod.reshape(mod.shape[0], 1, N_MOD * d)
    lat_row = lambda b: b
    ctx_row = lambda b: bsz
    g = norm_g[lyr][:, None, :]

    wi1, wo1 = w_ffn1_in[lyr].astype(BF16), w_ffn1_out[lyr].astype(BF16)

    x1, wi2, wo2, w_in_b, wfa_b, wfb_b, wout_b = _ffn(
        x, mod, lat_row, 0, g[0], g[1], wi1, wo1,
        casts=(w_ffn2_in[lyr], w_ffn2_out[lyr], w_in[lyr], w_fa[lyr], w_fb[lyr], w_out[lyr]), tm=1024)
    ctx_flat = ctx.reshape(1, bsz * ctx.shape[1], d)
    c1 = _ffn(ctx_flat, mod, ctx_row, 0, g[0], g[1], wi1, wo1)

    wgate = (0.5 * jnp.concatenate([w_r[lyr], w_i[lyr]], axis=-1)).astype(BF16)
    br, bi, lm = 0.5 * b_r[lyr][:, None, :], 0.5 * b_i[lyr][:, None, :], lam[lyr][:, None, :]
    cw, cb = conv_w[lyr], conv_b[lyr][None, :]
    h0f, h0b = _ctx(c1.reshape(ctx.shape), mod, bsz, 3, g[2], w_in_b, d_f, cw, cb, wgate, br, bi, lm)

    splits = ((0, d_f), (d_f, d_lru), (d_f + d_lru, d_lru), (d_f + 2 * d_lru, d), (d_f + 2 * d_lru + d, d))
    fx, lx, gx, gax, gbx, hf = _mid(x1, mod, 3, g[2], w_in_b, splits, cw, cb, wgate[0], br[0], bi[0], lm[0], h0f)

    a1, a2, d2 = _dft_tables(seq // GRID_W, GRID_W, d_f // FOURIER_GROUPS, FOURIER_GROUPS)
    fo = _fourier(fx, a1, a2, d2, GRID_W)

    return _tail(lx, cw, cb, wgate[1], br[1], bi[1], lm[1], h0b, x1, hf, gx, gax, gbx, fo, mod,
                 g[3], g[4], g[5], wfa_b, wfb_b, wout_b, wi2, wo2)
```

```python
import functools
import math

import numpy as np
import jax
import jax.numpy as jnp
from jax import lax
from jax.experimental import pallas as pl
from jax.experimental.pallas import tpu as pltpu

LANES = 128
SUBLANES = 8
MXU_DIM = 256
VMEM_LIMIT_BYTES = 56 * 1024 * 1024
TAIL_VMEM_LIMIT_BYTES = 62 * 1024 * 1024

GRID_W = 64
FOURIER_GROUPS = 4
CONV_W = 4
CONV_LEFT = (CONV_W - 1) // 2
GATE_C = 8.0
N_MOD = 9
EPS = 1e-6

BF16 = jnp.bfloat16
F32 = jnp.float32

SEG_LEN = 4
SCAN_ROWS = SEG_LEN * SUBLANES


def _sigmoid(x):
    return 0.5 * jnp.tanh(0.5 * x) + 0.5


def _silu(x):
    return x * _sigmoid(x)


def _gelu_tanh(x):
    c = math.sqrt(2.0 / math.pi)
    return 0.5 * x * (1.0 + jnp.tanh(c * (x + 0.044715 * (x * x * x))))


def _rms(x, g):
    return x * lax.rsqrt(jnp.mean(x * x, axis=-1, keepdims=True) + EPS) * g


def _resident(shape):
    nd = len(shape)
    return pl.BlockSpec(shape, lambda *_: (0,) * nd, pipeline_mode=pl.Buffered(1))


def _params(semantics, vmem=VMEM_LIMIT_BYTES):
    return pltpu.CompilerParams(dimension_semantics=semantics, vmem_limit_bytes=vmem)


def _interleave(major, minor):
    merged, done = [], 0
    for i, piece in enumerate(major):
        while done < len(minor) and done * len(major) <= i * len(minor):
            merged.append(minor[done])
            done += 1
        merged.append(piece)
    return merged + list(minor[done:])


def _ada_kernel(c_ref, w_ref, b_ref, o_ref):
    s = _silu(c_ref[...]).astype(BF16)
    o_ref[...] = jnp.dot(s, w_ref[...].astype(BF16), preferred_element_type=F32) + b_ref[...]


def _ada(cc, w, b, tn=1536):
    m, d = cc.shape
    n = w.shape[1]
    return pl.pallas_call(
        _ada_kernel,
        grid=(n // tn,),
        in_specs=[pl.BlockSpec((m, d), lambda j: (0, 0)),
                  pl.BlockSpec((d, tn), lambda j: (0, j)),
                  pl.BlockSpec((1, tn), lambda j: (0, j))],
        out_specs=pl.BlockSpec((m, tn), lambda j: (0, j)),
        out_shape=jax.ShapeDtypeStruct((m, n), F32),
        compiler_params=_params(("arbitrary",)),
        name="adaln",
    )(cc, w, b)


def _ffn_pre(x, sh, sc, gpre):
    return (_rms(x, gpre) * (1.0 + sc) + sh).astype(BF16)


def _ffn_hidden(h, wi_ref, act_ref, chunk, between=None):
    dff = act_ref.shape[1]
    for i, c in enumerate(range(0, dff, chunk)):
        gate = jnp.dot(h[...], wi_ref[:, c:c + chunk], preferred_element_type=F32)
        up = jnp.dot(h[...], wi_ref[:, dff + c:dff + c + chunk], preferred_element_type=F32)
        act_ref[:, c:c + chunk] = (_silu(gate) * up).astype(BF16)
        if between is not None:
            between(i, dff // chunk)


def _ffn_post(x, ga, gpost, act_ref, wo_ref):
    y = jnp.dot(act_ref[...], wo_ref[...], preferred_element_type=F32)
    return x + (0.5 * ga) * _rms(y, gpost)


def _ffn_kernel(x_ref, sh_ref, sc_ref, ga_ref, gpre_ref, gpost_ref, wi_ref, wo_ref, *rest, chunk, n_cast):
    cast_in, o_ref, cast_out, act_ref = rest[:n_cast], rest[n_cast], rest[n_cast + 1:-1], rest[-1]
    x = x_ref[0]
    _ffn_hidden(_ffn_pre(x, sh_ref[0], sc_ref[0], gpre_ref[...]), wi_ref, act_ref, chunk)
    o_ref[0] = _ffn_post(x, ga_ref[0], gpost_ref[...], act_ref, wo_ref)
    for src, dst in zip(cast_in, cast_out):
        dst[...] = src[...].astype(dst.dtype)


def _ffn(x, mod, mod_row, k_shift, g_pre, g_post, wi, wo, casts=(), tm=512, chunk=MXU_DIM):
    bsz, seq, d = x.shape
    dff = wo.shape[0]
    nt = seq // tm
    n_steps = bsz * nt
    mod_spec = lambda k: pl.BlockSpec((1, 1, d), lambda b, i: (mod_row(b), 0, k))

    def cast_spec(w):
        rows = w.shape[0]
        n_blocks = max(n for n in range(1, n_steps + 1)
                       if n_steps % n == 0 and rows % n == 0 and (rows // n) % (2 * SUBLANES) == 0)
        rep = n_steps // n_blocks
        return pl.BlockSpec((rows // n_blocks, w.shape[1]), lambda b, i: ((b * nt + i) // rep, 0))

    cast_specs = lambda: [cast_spec(w) for w in casts]
    out = pl.pallas_call(
        functools.partial(_ffn_kernel, chunk=chunk, n_cast=len(casts)),
        grid=(bsz, nt),
        in_specs=[pl.BlockSpec((1, tm, d), lambda b, i: (b, i, 0)),
                  mod_spec(k_shift), mod_spec(k_shift + 1), mod_spec(k_shift + 2),
                  _resident((1, d)), _resident((1, d)),
                  _resident(wi.shape), _resident(wo.shape)] + cast_specs(),
        out_specs=[pl.BlockSpec((1, tm, d), lambda b, i: (b, i, 0))] + cast_specs(),
        out_shape=[jax.ShapeDtypeStruct(x.shape, F32)] + [jax.ShapeDtypeStruct(w.shape, BF16) for w in casts],
        scratch_shapes=[pltpu.VMEM((tm, dff), BF16)],
        compiler_params=_params(("arbitrary", "arbitrary")),
        name="ffn",
    )(x, mod, mod, mod, g_pre, g_post, wi, wo, *casts)
    return out if casts else out[0]


def _lru_c1(lam):
    e = jnp.exp(-jnp.abs(lam))
    u = 1.0 + e
    log1p_e = jnp.where(u == 1.0, e, jnp.log(u) * (e / (u - 1.0)))
    return (-0.5 * GATE_C * math.log2(math.e)) * (jnp.maximum(-lam, 0.0) + log1p_e)


def _lru_fill_ext(ext_ref, j, prev, main, nxt, at_start, at_end):
    tt = main.shape[0]
    ext_ref[j, 0:SUBLANES, :] = jnp.where(at_start, 0.0, prev)
    ext_ref[j, SUBLANES:SUBLANES + tt, :] = main
    ext_ref[j, SUBLANES + tt:2 * SUBLANES + tt, :] = jnp.where(at_end, 0.0, nxt)


def _lru_conv(ext_ref, j, row0, nrows, cw, cb):
    xc = cb
    for k in range(CONV_W):
        off = SUBLANES - CONV_LEFT + k + row0
        tap = jnp.concatenate(
            [ext_ref[j, pl.ds(off + m + kk, SUBLANES, stride=SEG_LEN), :]
             for m in range(0, nrows, SCAN_ROWS) for kk in range(SEG_LEN)], axis=0)
        xc = xc + tap * cw[k:k + 1, :]
    return xc


def _lru_gate_math(xc, wg, br, bi, c1):
    z = jnp.dot(xc.astype(BF16), wg, preferred_element_type=F32)
    t_r = jnp.tanh(z[:, :LANES] + br)
    t_i = jnp.tanh(z[:, LANES:] + bi)
    a = jnp.exp2(c1 * t_r + c1)
    y = 1.0 - a * a
    root = jnp.where(y > 0.0, y * lax.rsqrt(y), 0.0)
    hx = 0.5 * xc
    return a, root * (t_i * hx + hx)


def _shift_rows(x, k, fill, rows):
    return jnp.where(rows >= k, pltpu.roll(x, k, axis=0), fill)


def _shift_rows_up(x, k, fill, rows):
    return jnp.where(rows < SUBLANES - k, pltpu.roll(x, SUBLANES - k, axis=0), fill)


def _scan_group(a_all, b_all, base, o_ref, j, out_base, carry, rows, reverse):
    ks = range(SEG_LEN - 1, -1, -1) if reverse else range(SEG_LEN)
    shift = _shift_rows_up if reverse else _shift_rows
    a = {k: a_all[base + k * SUBLANES:base + (k + 1) * SUBLANES, :] for k in ks}
    b = {k: b_all[base + k * SUBLANES:base + (k + 1) * SUBLANES, :] for k in ks}
    h, p = {}, {}
    prev = None
    for k in ks:
        if prev is None:
            h[k], p[k] = b[k], a[k]
        else:
            h[k] = a[k] * h[prev] + b[k]
            p[k] = a[k] * p[prev]
        prev = k
    hc, pc = h[prev], p[prev]
    for dd in (1, 2, 4):
        hc = pc * shift(hc, dd, 0.0, rows) + hc
        pc = pc * shift(pc, dd, 1.0, rows)
    g = hc + pc * carry
    if o_ref is not None:
        cin = shift(g, 1, carry, rows)
        for k in ks:
            o_ref[j, pl.ds(out_base + k, SUBLANES, stride=SEG_LEN), :] = h[k] + p[k] * cin
    last = 0 if reverse else SUBLANES - 1
    return jnp.broadcast_to(g[last:last + 1, :], (SUBLANES, LANES))


def _ctx_kernel(c_ref, sh_ref, sc_ref, g_ref, w_ref, cw_ref, cb_ref, wg_ref, br_ref, bi_ref, lam_ref,
                hf_ref, hb_ref, ext_ref, *, l0):
    tt = c_ref.shape[1]
    n_slab = ext_ref.shape[0]
    h = (_rms(c_ref[0], g_ref[...]) * (1.0 + sc_ref[0]) + sh_ref[0]).astype(BF16)
    halo = jnp.zeros((SUBLANES, LANES), F32)
    rows = lax.broadcasted_iota(jnp.int32, (SUBLANES, LANES), 0)
    for j0 in range(0, n_slab, MXU_DIM // LANES):
        cols = slice(l0 + j0 * LANES, l0 + j0 * LANES + MXU_DIM)
        l_pair = jnp.dot(h, w_ref[:, cols], preferred_element_type=F32)
        for j in range(j0, j0 + MXU_DIM // LANES):
            sl = slice(j * LANES, (j + 1) * LANES)
            _lru_fill_ext(ext_ref, j, halo, l_pair[:, (j - j0) * LANES:(j - j0 + 1) * LANES], halo, False, False)
            xc = _lru_conv(ext_ref, j, 0, tt, cw_ref[:, sl], cb_ref[:, sl])
            for d, (o_ref, reverse) in enumerate(((hf_ref, False), (hb_ref, True))):
                a, b = _lru_gate_math(xc, wg_ref[d, j], br_ref[d, :, sl], bi_ref[d, :, sl],
                                      _lru_c1(lam_ref[d, :, sl]))
                carry = jnp.zeros((SUBLANES, LANES), F32)
                for base in (range(tt - SCAN_ROWS, -1, -SCAN_ROWS) if reverse else range(0, tt, SCAN_ROWS)):
                    carry = _scan_group(a, b, base, None, j, base, carry, rows, reverse)
                o_ref[0, :, sl] = carry[0:1, :]


def _ctx(c1, mod, mod_row, k_shift, g, w, l0, conv_w, conv_b, wg, br, bi, lam):
    bsz, tt, d = c1.shape
    n_slab = wg.shape[1]
    ch = n_slab * LANES
    mod_spec = lambda k: pl.BlockSpec((1, 1, d), lambda b: (mod_row, 0, k))
    end_state = lambda: pl.BlockSpec((1, 1, ch), lambda b: (b, 0, 0))
    return pl.pallas_call(
        functools.partial(_ctx_kernel, l0=l0),
        grid=(bsz,),
        in_specs=[pl.BlockSpec((1, tt, d), lambda b: (b, 0, 0)), mod_spec(k_shift), mod_spec(k_shift + 1),
                  _resident((1, d)), _resident(w.shape), _resident(conv_w.shape), _resident((1, ch)),
                  _resident(wg.shape), _resident(br.shape), _resident(bi.shape), _resident(lam.shape)],
        out_specs=[end_state(), end_state()],
        out_shape=[jax.ShapeDtypeStruct((bsz, 1, ch), F32)] * 2,
        scratch_shapes=[pltpu.VMEM((n_slab, tt + 2 * SUBLANES, LANES), F32)],
        compiler_params=_params(("arbitrary",)),
        name="ctx_scan",
    )(c1, mod, mod, g, w, conv_w, conv_b, wg, br, bi, lam)


def _halo_specs(tt, t, ch, idx):
    hb = tt // SUBLANES
    last = t // SUBLANES - 1

    def prev_map(*g):
        b, tc = idx(*g)
        return b, jnp.maximum(tc * hb - 1, 0), 0

    def next_map(*g):
        b, tc = idx(*g)
        return b, jnp.minimum((tc + 1) * hb, last), 0

    return [pl.BlockSpec((1, tt, ch), lambda *g: idx(*g) + (0,)),
            pl.BlockSpec((1, SUBLANES, ch), prev_map),
            pl.BlockSpec((1, SUBLANES, ch), next_map)]


COL_GROUP = MXU_DIM
ROW_GROUP = SUBLANES


def _dft_tables(rows, cols, group_w, groups):
    scale = 1.0 / math.sqrt(rows * cols * group_w)
    ang = lambda n: 2.0 * np.pi * ((np.arange(n)[:, None] * np.arange(n)[None, :]) % n) / n
    eye = np.eye
    c_col, s_col = np.kron(eye(COL_GROUP // cols), np.cos(ang(cols))), np.kron(eye(COL_GROUP // cols), np.sin(ang(cols)))
    a1 = np.concatenate([c_col, -s_col], axis=0)
    c_row, s_row = np.kron(np.cos(ang(rows)), eye(ROW_GROUP)) * scale, np.kron(np.sin(ang(rows)), eye(ROW_GROUP)) * scale
    a2 = np.block([[c_row, s_row], [-s_row, c_row]])
    d2 = np.concatenate([np.kron(eye(groups), np.cos(ang(group_w))), np.kron(eye(groups), np.sin(ang(group_w)))], axis=0)
    as_bf16 = lambda m: jnp.asarray(m.astype(np.float32)).astype(BF16)
    return as_bf16(a1), as_bf16(a2), as_bf16(d2)


def _fourier_kernel(f_ref, a1_ref, a2_ref, d2_ref, o_ref, yre_ref, yim_ref, p_ref, q_ref, *, cols, tm):
    n_tok, df = f_ref.shape[1], f_ref.shape[2]
    rows = n_tok // cols
    for t0 in range(0, n_tok, COL_GROUP):
        y = jnp.dot(a1_ref[...], f_ref[0, t0:t0 + COL_GROUP, :], preferred_element_type=F32)
        yre_ref[t0:t0 + COL_GROUP, :] = y[:COL_GROUP]
        yim_ref[t0:t0 + COL_GROUP, :] = y[COL_GROUP:]
    n_grp = rows * ROW_GROUP
    for c0 in range(0, cols, ROW_GROUP):
        slabs = [slice(r * cols + c0, r * cols + c0 + ROW_GROUP) for r in range(rows)]
        y = jnp.concatenate([yre_ref[sl, :] for sl in slabs] + [yim_ref[sl, :] for sl in slabs], axis=0)
        z = jnp.dot(a2_ref[...], y.astype(BF16), preferred_element_type=F32)
        for r, sl in enumerate(slabs):
            p_ref[sl, :] = z[r * ROW_GROUP:(r + 1) * ROW_GROUP]
            q_ref[sl, :] = z[n_grp + r * ROW_GROUP:n_grp + (r + 1) * ROW_GROUP]
    for m in range(0, n_tok, tm):
        o = jnp.dot(p_ref[m:m + tm, :].astype(BF16), d2_ref[:df, :], preferred_element_type=F32)
        o = o + jnp.dot(q_ref[m:m + tm, :].astype(BF16), d2_ref[df:, :], preferred_element_type=F32)
        o_ref[0, m:m + tm, :] = o.astype(o_ref.dtype)


def _fourier(f, a1, a2, d2, cols, tm=512):
    bsz, n_tok, df = f.shape
    return pl.pallas_call(
        functools.partial(_fourier_kernel, cols=cols, tm=tm),
        grid=(bsz,),
        in_specs=[pl.BlockSpec((1, n_tok, df), lambda b: (b, 0, 0)),
                  _resident(a1.shape), _resident(a2.shape), _resident(d2.shape)],
        out_specs=pl.BlockSpec((1, n_tok, df), lambda b: (b, 0, 0)),
        out_shape=jax.ShapeDtypeStruct(f.shape, BF16),
        scratch_shapes=[pltpu.VMEM((n_tok, df), F32) for _ in range(4)],
        compiler_params=_params(("arbitrary",)),
        name="fourier",
    )(f, a1, a2, d2)


def _mid_kernel(x_ref, sh_ref, sc_ref, g_ref, w_ref, cw_ref, cb_ref, wg_ref, br_ref, bi_ref, lam_ref, h0_ref,
                f_ref, l_ref, gt_ref, hf_ref, ext_ref, carry_ref,
                *, tm, nt, n_tiles, half, chunk, splits):
    s = pl.program_id(0)
    ip = jnp.minimum(s, n_tiles - 1) % nt
    ir = jnp.maximum(s - 1, 0) % nt
    n_slab = ext_ref.shape[0]
    (f0, fw), (l0, lw), (g0, gw), (ga0, gaw), (gb0, gbw) = splits

    @pl.when(s == 0)
    def _():
        ext_ref[...] = jnp.zeros(ext_ref.shape, F32)
        carry_ref[...] = jnp.zeros(carry_ref.shape, F32)

    h = (_rms(x_ref[0], g_ref[...]) * (1.0 + sc_ref[0]) + sh_ref[0]).astype(BF16)

    def project(o_ref, dst, start, c, act):
        r = jnp.dot(h, w_ref[:, start + c:start + c + chunk], preferred_element_type=F32)
        o_ref[0, :, dst + c:dst + c + chunk] = (r if act is None else act(r)).astype(o_ref.dtype)

    c1 = _lru_c1(lam_ref[...])
    rows = lax.broadcasted_iota(jnp.int32, (SUBLANES, LANES), 0)
    carries = []
    for j in range(n_slab):
        sl = slice(j * LANES, (j + 1) * LANES)
        h0 = jnp.broadcast_to(h0_ref[0, :, sl], (SUBLANES, LANES))
        carries.append(jnp.where(ir == 0, h0, carry_ref[j]))

    def close_window():
        for j in range(n_slab):
            sl = slice(j * LANES, (j + 1) * LANES)
            ext_ref[j, SUBLANES + tm:2 * SUBLANES + tm, :] = jnp.where(ir == nt - 1, 0.0, l_ref[0, 0:SUBLANES, sl])

    xcs, gates = {}, {}

    def conv_piece(row0, j):
        sl = slice(j * LANES, (j + 1) * LANES)
        xcs[row0, j] = _lru_conv(ext_ref, j, row0, half, cw_ref[:, sl], cb_ref[:, sl])

    def gate_piece(row0, j):
        sl = slice(j * LANES, (j + 1) * LANES)
        gates[row0, j] = _lru_gate_math(xcs.pop((row0, j)), wg_ref[j], br_ref[:, sl], bi_ref[:, sl], c1[:, sl])

    def scan_piece(row0, j):
        a, b = gates.pop((row0, j))
        for base in range(0, half, SCAN_ROWS):
            carries[j] = _scan_group(a, b, base, hf_ref.at[0], j, row0 + base, carries[j], rows, False)

    scan_pieces = []
    for row0 in range(0, tm, half):
        if row0 + half == tm:
            scan_pieces.append(close_window)
        scan_pieces.append(functools.partial(conv_piece, row0, 0))
        for j in range(n_slab):
            if j + 1 < n_slab:
                scan_pieces.append(functools.partial(conv_piece, row0, j + 1))
            scan_pieces.append(functools.partial(gate_piece, row0, j))
            scan_pieces.append(functools.partial(scan_piece, row0, j))

    proj_pieces = [functools.partial(project, o_ref, dst, start, c, act)
                   for o_ref, dst, start, width, act in (
                       (l_ref, 0, l0, lw, None), (f_ref, 0, f0, fw, None), (gt_ref, 0, g0, gw, _gelu_tanh),
                       (gt_ref, gw, ga0, gaw, _sigmoid), (gt_ref, gw + gaw, gb0, gbw, _sigmoid))
                   for c in range(0, width, chunk)]
    assert half < tm and lw // chunk <= len(proj_pieces) // 2
    for piece in _interleave(scan_pieces, proj_pieces):
        piece()

    for j in range(n_slab):
        sl = slice(j * LANES, (j + 1) * LANES)
        carry_ref[j] = carries[j]
        ext_ref[j, 0:SUBLANES, :] = jnp.where(ip == 0, 0.0, ext_ref[j, tm:tm + SUBLANES, :])
        ext_ref[j, SUBLANES:SUBLANES + tm, :] = l_ref[0, :, sl]


def _mid(x, mod, k_shift, g, w, splits, conv_w, conv_b, wg, br, bi, lam, h0, tm=512, half=128, chunk=MXU_DIM):
    bsz, seq, d = x.shape
    ch = splits[1][1]
    nt = seq // tm
    n_tiles = bsz * nt
    n_slab = ch // LANES

    def proj_tile(s):
        p = jnp.minimum(s, n_tiles - 1)
        return p // nt, p % nt

    def scan_tile(s):
        r = jnp.maximum(s - 1, 0)
        return r // nt, r % nt

    tok = lambda wd: pl.BlockSpec((1, tm, wd), lambda s: proj_tile(s) + (0,))
    mod_spec = lambda k: pl.BlockSpec((1, 1, d), lambda s: (proj_tile(s)[0], 0, k))
    widths = [splits[0][1], splits[1][1], sum(wd for _, wd in splits[2:])]
    return pl.pallas_call(
        functools.partial(_mid_kernel, tm=tm, nt=nt, n_tiles=n_tiles, half=half, chunk=chunk, splits=splits),
        grid=(n_tiles + 1,),
        in_specs=[tok(d), mod_spec(k_shift), mod_spec(k_shift + 1), _resident((1, d)), _resident(w.shape),
                  _resident(conv_w.shape), _resident((1, ch)), _resident(wg.shape),
                  _resident((1, ch)), _resident((1, ch)), _resident((1, ch)),
                  pl.BlockSpec((1, 1, ch), lambda s: (scan_tile(s)[0], 0, 0))],
        out_specs=[tok(wd) for wd in widths] + [
            pl.BlockSpec((1, n_slab, tm, LANES), lambda s: (scan_tile(s)[0], 0, scan_tile(s)[1], 0))],
        out_shape=[jax.ShapeDtypeStruct((bsz, seq, wd), dt)
                   for wd, dt in zip(widths, (BF16, F32, BF16))] + [
            jax.ShapeDtypeStruct((bsz, n_slab, seq, LANES), F32)],
        scratch_shapes=[pltpu.VMEM((n_slab, tm + 2 * SUBLANES, LANES), F32),
                        pltpu.VMEM((n_slab, SUBLANES, LANES), F32)],
        compiler_params=_params(("arbitrary",)),
        name="mid",
    )(x, mod, mod, g, w, conv_w, conv_b, wg, br, bi, lam, h0)


def _tail_kernel(l_ref, lp_ref, ln_ref, cw_ref, cb_ref, wg_ref, br_ref, bi_ref, lam_ref, h0_ref,
                 x_ref, hf_ref, gt_ref, fo_ref,
                 gate2_ref, sh3_ref, sc3_ref, gate3_ref, gn_ref, gpre_ref, gpost_ref,
                 wfa_ref, wfb_ref, wo_ref, wi2_ref, wo2_ref,
                 o_ref, ext_ref, hb_ref, carry_ref, act_ref,
                 *, tm, nt, n_tiles, half, chunk, merge_split):
    s = pl.program_id(0)
    q = jnp.minimum(s, n_tiles - 1)
    cq = nt - 1 - q % nt
    n_slab = ext_ref.shape[0]
    ch, d = n_slab * LANES, x_ref.shape[2]

    @pl.when(s == 0)
    def _():
        hb_ref[...] = jnp.zeros(hb_ref.shape, F32)
        carry_ref[...] = jnp.zeros(carry_ref.shape, F32)

    blocks = [slice(r0, r0 + tm // merge_split) for r0 in range(0, tm, tm // merge_split)]
    yas = [jnp.dot(fo_ref[0, rs, :], wfa_ref[...], preferred_element_type=F32) for rs in blocks]
    ybs = []
    for rs in blocks:
        hsum = jnp.concatenate([hf_ref[0, j, rs, :] + hb_ref[j, rs, :] for j in range(n_slab)], axis=1)
        u = (hsum * gt_ref[0, rs, 0:ch].astype(F32)).astype(BF16)
        ybs.append(jnp.dot(u, wfb_ref[...], preferred_element_type=F32))
    mxs = []
    for rs, ya, yb in zip(blocks, yas, ybs):
        m = gt_ref[0, rs, ch:ch + d].astype(F32) * ya + gt_ref[0, rs, ch + d:ch + 2 * d].astype(F32) * yb
        mxs.append(jnp.dot(m.astype(BF16), wo_ref[...], preferred_element_type=F32))
    x2s = [x_ref[0, rs, :] + gate2_ref[0] * _rms(mx, gn_ref[...]) for rs, mx in zip(blocks, mxs)]
    hs = [_ffn_pre(x2, sh3_ref[0], sc3_ref[0], gpre_ref[...]) for x2 in x2s]
    x2 = jnp.concatenate(x2s, axis=0)
    h2 = jnp.concatenate(hs, axis=0)

    c1 = _lru_c1(lam_ref[...])
    rows = lax.broadcasted_iota(jnp.int32, (SUBLANES, LANES), 0)
    carries = []
    for j in range(n_slab):
        sl = slice(j * LANES, (j + 1) * LANES)
        _lru_fill_ext(ext_ref, j, lp_ref[0, :, sl], l_ref[0, :, sl], ln_ref[0, :, sl], cq == 0, cq == nt - 1)
        h0 = jnp.broadcast_to(h0_ref[0, :, sl], (SUBLANES, LANES))
        carries.append(jnp.where(cq == nt - 1, h0, carry_ref[j]))

    xcs, gates = {}, {}

    def conv_piece(row0, j):
        sl = slice(j * LANES, (j + 1) * LANES)
        xcs[row0, j] = _lru_conv(ext_ref, j, row0, half, cw_ref[:, sl], cb_ref[:, sl])

    def gate_piece(row0, j):
        sl = slice(j * LANES, (j + 1) * LANES)
        gates[row0, j] = _lru_gate_math(xcs.pop((row0, j)), wg_ref[j], br_ref[:, sl], bi_ref[:, sl], c1[:, sl])

    def scan_piece(row0, j):
        a, b = gates.pop((row0, j))
        for base in range(half - SCAN_ROWS, -1, -SCAN_ROWS):
            carries[j] = _scan_group(a, b, base, hb_ref, j, row0 + base, carries[j], rows, True)

    pieces = []
    for row0 in range(tm - half, -1, -half):
        pieces.append(functools.partial(conv_piece, row0, 0))
        for j in range(n_slab):
            if j + 1 < n_slab:
                pieces.append(functools.partial(conv_piece, row0, j + 1))
            pieces.append(functools.partial(gate_piece, row0, j))
            pieces.append(functools.partial(scan_piece, row0, j))

    def between(i, n):
        for piece in pieces[i * len(pieces) // n:(i + 1) * len(pieces) // n]:
            piece()

    _ffn_hidden(h2, wi2_ref, act_ref, chunk, between)
    o_ref[0] = _ffn_post(x2, gate3_ref[0], gpost_ref[...], act_ref, wo2_ref)
    for j in range(n_slab):
        carry_ref[j] = carries[j]


def _tail(l, conv_w, conv_b, wg, br, bi, lam, h0, x, hf, gates, fo, mod, gn, gpre, gpost,
          wfa, wfb, wo, wi2, wo2, tm=512, half=128, chunk=MXU_DIM, merge_split=2):
    bsz, seq, d = x.shape
    ch = l.shape[2]
    nt = seq // tm
    n_tiles = bsz * nt
    n_slab = ch // LANES
    dff = wo2.shape[0]

    def scan_tile(s):
        q = jnp.minimum(s, n_tiles - 1)
        return q // nt, nt - 1 - q % nt

    def merge_tile(s):
        r = jnp.clip(s - 1, 0, n_tiles - 1)
        return r // nt, nt - 1 - r % nt

    tok = lambda w: pl.BlockSpec((1, tm, w), lambda s: merge_tile(s) + (0,))
    mod_spec = lambda k: pl.BlockSpec((1, 1, d), lambda s: (merge_tile(s)[0], 0, k))
    return pl.pallas_call(
        functools.partial(_tail_kernel, tm=tm, nt=nt, n_tiles=n_tiles, half=half, chunk=chunk,
                          merge_split=merge_split),
        grid=(n_tiles + 1,),
        in_specs=_halo_specs(tm, seq, ch, scan_tile) + [
            _resident(conv_w.shape), _resident((1, ch)), _resident(wg.shape),
            _resident((1, ch)), _resident((1, ch)), _resident((1, ch)),
            pl.BlockSpec((1, 1, ch), lambda s: (scan_tile(s)[0], 0, 0)),
            tok(d),
            pl.BlockSpec((1, n_slab, tm, LANES), lambda s: (merge_tile(s)[0], 0, merge_tile(s)[1], 0)),
            tok(gates.shape[2]), tok(fo.shape[2]),
            mod_spec(5), mod_spec(6), mod_spec(7), mod_spec(8),
            _resident((1, d)), _resident((1, d)), _resident((1, d)),
            _resident(wfa.shape), _resident(wfb.shape), _resident(wo.shape),
            _resident(wi2.shape), _resident(wo2.shape)],
        out_specs=tok(d),
        out_shape=jax.ShapeDtypeStruct(x.shape, F32),
        scratch_shapes=[pltpu.VMEM((n_slab, tm + 2 * SUBLANES, LANES), F32),
                        pltpu.VMEM((n_slab, tm, LANES), F32),
                        pltpu.VMEM((n_slab, SUBLANES, LANES), F32),
                        pltpu.VMEM((tm, dff), BF16)],
        compiler_params=_params(("arbitrary",), TAIL_VMEM_LIMIT_BYTES),
        name="tail",
    )(l, l, l, conv_w, conv_b, wg, br, bi, lam, h0,
      x, hf, gates, fo, mod, mod, mod, mod, gn, gpre, gpost, wfa, wfb, wo, wi2, wo2)


def kernel(x, c, ctx, c_ctx, w_ada, b_ada, norm_g, w_ffn1_in, w_ffn1_out, w_ffn2_in, w_ffn2_out,
           w_in, conv_w, conv_b, w_r, b_r, w_i, b_i, lam, w_fa, w_fb, w_out):
    depth = w_ada.shape[0]
    assert depth == 1, "single-layer problem: the context stream is only needed up to the mixer scans"
    bsz, seq, d = x.shape
    d_lru = w_fb.shape[1]
    d_f = w_fa.shape[1]
    lyr = 0

    pad = (-(bsz + 1)) % SUBLANES
    cc = jnp.concatenate([c, c_ctx[None, :], jnp.zeros((pad, d), F32)], axis=0)
    mod = _ada(cc, w_ada[lyr], b_ada[lyr][None, :])
    mod = mod.reshape(mod.shape[0], 1, N_MOD * d)
    lat_row = lambda b: b
    ctx_row = lambda b: bsz
    g = norm_g[lyr][:, None, :]

    wi1, wo1 = w_ffn1_in[lyr].astype(BF16), w_ffn1_out[lyr].astype(BF16)

    x1, wi2, wo2, w_in_b, wfa_b, wfb_b, wout_b = _ffn(
        x, mod, lat_row, 0, g[0], g[1], wi1, wo1,
        casts=(w_ffn2_in[lyr], w_ffn2_out[lyr], w_in[lyr], w_fa[lyr], w_fb[lyr], w_out[lyr]), tm=1024)
    ctx_flat = ctx.reshape(1, bsz * ctx.shape[1], d)
    c1 = _ffn(ctx_flat, mod, ctx_row, 0, g[0], g[1], wi1, wo1)

    wgate = (0.5 * jnp.concatenate([w_r[lyr], w_i[lyr]], axis=-1)).astype(BF16)
    br, bi, lm = 0.5 * b_r[lyr][:, None, :], 0.5 * b_i[lyr][:, None, :], lam[lyr][:, None, :]
    cw, cb = conv_w[lyr], conv_b[lyr][None, :]
    h0f, h0b = _ctx(c1.reshape(ctx.shape), mod, bsz, 3, g[2], w_in_b, d_f, cw, cb, wgate, br, bi, lm)

    splits = ((0, d_f), (d_f, d_lru), (d_f + d_lru, d_lru), (d_f + 2 * d_lru, d), (d_f + 2 * d_lru + d, d))
    fx, lx, gates, hf = _mid(x1, mod, 3, g[2], w_in_b, splits, cw, cb, wgate[0], br[0], bi[0], lm[0], h0f)

    a1, a2, d2 = _dft_tables(seq // GRID_W, GRID_W, d_f // FOURIER_GROUPS, FOURIER_GROUPS)
    fo = _fourier(fx, a1, a2, d2, GRID_W)

    return _tail(lx, cw, cb, wgate[1], br[1], bi[1], lm[1], h0b, x1, hf, gates, fo, mod,
                 g[3], g[4], g[5], wfa_b, wfb_b, wout_b, wi2, wo2)
```

```python
import functools
import math

import numpy as np
import jax
import jax.numpy as jnp
from jax import lax
from jax.experimental import pallas as pl
from jax.experimental.pallas import tpu as pltpu

LANES = 128
SUBLANES = 8
MXU_DIM = 256
VMEM_LIMIT_BYTES = 56 * 1024 * 1024
TAIL_VMEM_LIMIT_BYTES = 62 * 1024 * 1024

GRID_W = 64
FOURIER_GROUPS = 4
CONV_W = 4
CONV_LEFT = (CONV_W - 1) // 2
GATE_C = 8.0
N_MOD = 9
EPS = 1e-6

BF16 = jnp.bfloat16
F32 = jnp.float32

SEG_LEN = 4
SCAN_ROWS = SEG_LEN * SUBLANES
EDGE = SCAN_ROWS


def _sigmoid(x):
    return 0.5 * jnp.tanh(0.5 * x) + 0.5


def _silu(x):
    return x * _sigmoid(x)


def _gelu_tanh(x):
    c = math.sqrt(2.0 / math.pi)
    return 0.5 * x * (1.0 + jnp.tanh(c * (x + 0.044715 * (x * x * x))))


def _rms(x, g):
    return x * lax.rsqrt(jnp.mean(x * x, axis=-1, keepdims=True) + EPS) * g


def _resident(shape):
    nd = len(shape)
    return pl.BlockSpec(shape, lambda *_: (0,) * nd, pipeline_mode=pl.Buffered(1))


def _params(semantics, vmem=VMEM_LIMIT_BYTES):
    return pltpu.CompilerParams(dimension_semantics=semantics, vmem_limit_bytes=vmem)


def _interleave(major, minor):
    merged, done = [], 0
    for i, piece in enumerate(major):
        while done < len(minor) and done * len(major) <= i * len(minor):
            merged.append(minor[done])
            done += 1
        merged.append(piece)
    return merged + list(minor[done:])


def _ada_kernel(c_ref, w_ref, b_ref, o_ref):
    s = _silu(c_ref[...]).astype(BF16)
    o_ref[...] = jnp.dot(s, w_ref[...].astype(BF16), preferred_element_type=F32) + b_ref[...]


def _ada(cc, w, b, tn=1536):
    m, d = cc.shape
    n = w.shape[1]
    return pl.pallas_call(
        _ada_kernel,
        grid=(n // tn,),
        in_specs=[pl.BlockSpec((m, d), lambda j: (0, 0)),
                  pl.BlockSpec((d, tn), lambda j: (0, j)),
                  pl.BlockSpec((1, tn), lambda j: (0, j))],
        out_specs=pl.BlockSpec((m, tn), lambda j: (0, j)),
        out_shape=jax.ShapeDtypeStruct((m, n), F32),
        compiler_params=_params(("arbitrary",)),
        name="adaln",
    )(cc, w, b)


def _ffn_pre(x, sh, sc, gpre):
    return (_rms(x, gpre) * (1.0 + sc) + sh).astype(BF16)


def _ffn_hidden(h, wi_ref, act_ref, chunk, between=None):
    dff = act_ref.shape[1]
    for i, c in enumerate(range(0, dff, chunk)):
        gate = jnp.dot(h[...], wi_ref[:, c:c + chunk], preferred_element_type=F32)
        up = jnp.dot(h[...], wi_ref[:, dff + c:dff + c + chunk], preferred_element_type=F32)
        act_ref[:, c:c + chunk] = (_silu(gate) * up).astype(BF16)
        if between is not None:
            between(i, dff // chunk)


def _ffn_post(x, ga, gpost, act_ref, wo_ref):
    y = jnp.dot(act_ref[...], wo_ref[...], preferred_element_type=F32)
    return x + (0.5 * ga) * _rms(y, gpost)


def _ffn_kernel(x_ref, sh_ref, sc_ref, ga_ref, gpre_ref, gpost_ref, wi_ref, wo_ref, *rest, chunk, n_cast):
    cast_in, o_ref, cast_out, act_ref = rest[:n_cast], rest[n_cast], rest[n_cast + 1:-1], rest[-1]
    x = x_ref[0]
    _ffn_hidden(_ffn_pre(x, sh_ref[0], sc_ref[0], gpre_ref[...]), wi_ref, act_ref, chunk)
    o_ref[0] = _ffn_post(x, ga_ref[0], gpost_ref[...], act_ref, wo_ref)
    for src, dst in zip(cast_in, cast_out):
        dst[...] = src[...].astype(dst.dtype)


def _ffn(x, mod, mod_row, k_shift, g_pre, g_post, wi, wo, casts=(), tm=512, chunk=MXU_DIM):
    bsz, seq, d = x.shape
    dff = wo.shape[0]
    nt = seq // tm
    n_steps = bsz * nt
    mod_spec = lambda k: pl.BlockSpec((1, 1, d), lambda b, i: (mod_row(b), 0, k))

    def cast_spec(w):
        rows = w.shape[0]
        n_blocks = max(n for n in range(1, n_steps + 1)
                       if n_steps % n == 0 and rows % n == 0 and (rows // n) % (2 * SUBLANES) == 0)
        rep = n_steps // n_blocks
        return pl.BlockSpec((rows // n_blocks, w.shape[1]), lambda b, i: ((b * nt + i) // rep, 0))

    cast_specs = lambda: [cast_spec(w) for w in casts]
    out = pl.pallas_call(
        functools.partial(_ffn_kernel, chunk=chunk, n_cast=len(casts)),
        grid=(bsz, nt),
        in_specs=[pl.BlockSpec((1, tm, d), lambda b, i: (b, i, 0)),
                  mod_spec(k_shift), mod_spec(k_shift + 1), mod_spec(k_shift + 2),
                  _resident((1, d)), _resident((1, d)),
                  _resident(wi.shape), _resident(wo.shape)] + cast_specs(),
        out_specs=[pl.BlockSpec((1, tm, d), lambda b, i: (b, i, 0))] + cast_specs(),
        out_shape=[jax.ShapeDtypeStruct(x.shape, F32)] + [jax.ShapeDtypeStruct(w.shape, BF16) for w in casts],
        scratch_shapes=[pltpu.VMEM((tm, dff), BF16)],
        compiler_params=_params(("arbitrary", "arbitrary")),
        name="ffn",
    )(x, mod, mod, mod, g_pre, g_post, wi, wo, *casts)
    return out if casts else out[0]


def _lru_c1(lam):
    e = jnp.exp(-jnp.abs(lam))
    u = 1.0 + e
    log1p_e = jnp.where(u == 1.0, e, jnp.log(u) * (e / (u - 1.0)))
    return (-0.5 * GATE_C * math.log2(math.e)) * (jnp.maximum(-lam, 0.0) + log1p_e)


def _lru_fill_ext(ext_ref, j, prev, main, nxt, at_start, at_end):
    tt = main.shape[0]
    ext_ref[j, 0:SUBLANES, :] = jnp.where(at_start, 0.0, prev)
    ext_ref[j, SUBLANES:SUBLANES + tt, :] = main
    ext_ref[j, SUBLANES + tt:2 * SUBLANES + tt, :] = jnp.where(at_end, 0.0, nxt)


def _lru_conv(ext_ref, j, row0, nrows, cw, cb, load=None):
    if load is None:
        load = lambda r: ext_ref[j, pl.ds(SUBLANES + r, SUBLANES, stride=SEG_LEN), :]
    xc = cb
    for k in range(CONV_W):
        tap = jnp.concatenate([load(row0 + m + kk + k - CONV_LEFT)
                               for m in range(0, nrows, SCAN_ROWS) for kk in range(SEG_LEN)], axis=0)
        xc = xc + tap * cw[k:k + 1, :]
    return xc


def _lru_gate_math(xc, wg, br, bi, c1):
    z = jnp.dot(xc.astype(BF16), wg, preferred_element_type=F32)
    t_r = jnp.tanh(z[:, :LANES] + br)
    t_i = jnp.tanh(z[:, LANES:] + bi)
    a = jnp.exp2(c1 * t_r + c1)
    y = 1.0 - a * a
    root = jnp.where(y > 0.0, y * lax.rsqrt(y), 0.0)
    hx = 0.5 * xc
    return a, root * (t_i * hx + hx)


def _shift_rows(x, k, fill, rows):
    return jnp.where(rows >= k, pltpu.roll(x, k, axis=0), fill)


def _shift_rows_up(x, k, fill, rows):
    return jnp.where(rows < SUBLANES - k, pltpu.roll(x, SUBLANES - k, axis=0), fill)


def _scan_group(a_all, b_all, base, o_ref, j, out_base, carry, rows, reverse):
    ks = range(SEG_LEN - 1, -1, -1) if reverse else range(SEG_LEN)
    shift = _shift_rows_up if reverse else _shift_rows
    a = {k: a_all[base + k * SUBLANES:base + (k + 1) * SUBLANES, :] for k in ks}
    b = {k: b_all[base + k * SUBLANES:base + (k + 1) * SUBLANES, :] for k in ks}
    h, p = {}, {}
    prev = None
    for k in ks:
        if prev is None:
            h[k], p[k] = b[k], a[k]
        else:
            h[k] = a[k] * h[prev] + b[k]
            p[k] = a[k] * p[prev]
        prev = k
    hc, pc = h[prev], p[prev]
    for dd in (1, 2, 4):
        hc = pc * shift(hc, dd, 0.0, rows) + hc
        pc = pc * shift(pc, dd, 1.0, rows)
    g = hc + pc * carry
    if o_ref is not None:
        cin = shift(g, 1, carry, rows)
        for k in ks:
            o_ref[j, pl.ds(out_base + k, SUBLANES, stride=SEG_LEN), :] = h[k] + p[k] * cin
    last = 0 if reverse else SUBLANES - 1
    return jnp.broadcast_to(g[last:last + 1, :], (SUBLANES, LANES))


def _ctx_kernel(c_ref, sh_ref, sc_ref, g_ref, w_ref, cw_ref, cb_ref, wg_ref, br_ref, bi_ref, lam_ref,
                hf_ref, hb_ref, ext_ref, *, l0):
    tt = c_ref.shape[1]
    n_slab = ext_ref.shape[0]
    h = (_rms(c_ref[0], g_ref[...]) * (1.0 + sc_ref[0]) + sh_ref[0]).astype(BF16)
    halo = jnp.zeros((SUBLANES, LANES), F32)
    rows = lax.broadcasted_iota(jnp.int32, (SUBLANES, LANES), 0)
    for j0 in range(0, n_slab, MXU_DIM // LANES):
        cols = slice(l0 + j0 * LANES, l0 + j0 * LANES + MXU_DIM)
        l_pair = jnp.dot(h, w_ref[:, cols], preferred_element_type=F32)
        for j in range(j0, j0 + MXU_DIM // LANES):
            sl = slice(j * LANES, (j + 1) * LANES)
            _lru_fill_ext(ext_ref, j, halo, l_pair[:, (j - j0) * LANES:(j - j0 + 1) * LANES], halo, False, False)
            xc = _lru_conv(ext_ref, j, 0, tt, cw_ref[:, sl], cb_ref[:, sl])
            for d, (o_ref, reverse) in enumerate(((hf_ref, False), (hb_ref, True))):
                a, b = _lru_gate_math(xc, wg_ref[d, j], br_ref[d, :, sl], bi_ref[d, :, sl],
                                      _lru_c1(lam_ref[d, :, sl]))
                carry = jnp.zeros((SUBLANES, LANES), F32)
                for base in (range(tt - SCAN_ROWS, -1, -SCAN_ROWS) if reverse else range(0, tt, SCAN_ROWS)):
                    carry = _scan_group(a, b, base, None, j, base, carry, rows, reverse)
                o_ref[0, :, sl] = carry[0:1, :]


def _ctx(c1, mod, mod_row, k_shift, g, w, l0, conv_w, conv_b, wg, br, bi, lam):
    bsz, tt, d = c1.shape
    n_slab = wg.shape[1]
    ch = n_slab * LANES
    mod_spec = lambda k: pl.BlockSpec((1, 1, d), lambda b: (mod_row, 0, k))
    end_state = lambda: pl.BlockSpec((1, 1, ch), lambda b: (b, 0, 0))
    return pl.pallas_call(
        functools.partial(_ctx_kernel, l0=l0),
        grid=(bsz,),
        in_specs=[pl.BlockSpec((1, tt, d), lambda b: (b, 0, 0)), mod_spec(k_shift), mod_spec(k_shift + 1),
                  _resident((1, d)), _resident(w.shape), _resident(conv_w.shape), _resident((1, ch)),
                  _resident(wg.shape), _resident(br.shape), _resident(bi.shape), _resident(lam.shape)],
        out_specs=[end_state(), end_state()],
        out_shape=[jax.ShapeDtypeStruct((bsz, 1, ch), F32)] * 2,
        scratch_shapes=[pltpu.VMEM((n_slab, tt + 2 * SUBLANES, LANES), F32)],
        compiler_params=_params(("arbitrary",)),
        name="ctx_scan",
    )(c1, mod, mod, g, w, conv_w, conv_b, wg, br, bi, lam)


def _halo_specs(tt, t, ch, idx):
    hb = tt // SUBLANES
    last = t // SUBLANES - 1
    n_slab = ch // LANES

    def main_map(*g):
        b, tc = idx(*g)
        return b, 0, tc, 0

    def prev_map(*g):
        b, tc = idx(*g)
        return b, 0, jnp.maximum(tc * hb - 1, 0), 0

    def next_map(*g):
        b, tc = idx(*g)
        return b, 0, jnp.minimum((tc + 1) * hb, last), 0

    return [pl.BlockSpec((1, n_slab, tt, LANES), main_map),
            pl.BlockSpec((1, n_slab, SUBLANES, LANES), prev_map),
            pl.BlockSpec((1, n_slab, SUBLANES, LANES), next_map)]


COL_GROUP = MXU_DIM
ROW_GROUP = SUBLANES


def _dft_tables(rows, cols, group_w, groups):
    scale = 1.0 / math.sqrt(rows * cols * group_w)
    ang = lambda n: 2.0 * np.pi * ((np.arange(n)[:, None] * np.arange(n)[None, :]) % n) / n
    eye = np.eye
    c_col, s_col = np.kron(eye(COL_GROUP // cols), np.cos(ang(cols))), np.kron(eye(COL_GROUP // cols), np.sin(ang(cols)))
    a1 = np.concatenate([c_col, -s_col], axis=0)
    c_row, s_row = np.kron(np.cos(ang(rows)), eye(ROW_GROUP)) * scale, np.kron(np.sin(ang(rows)), eye(ROW_GROUP)) * scale
    a2 = np.block([[c_row, s_row], [-s_row, c_row]])
    d2 = np.concatenate([np.kron(eye(groups), np.cos(ang(group_w))), np.kron(eye(groups), np.sin(ang(group_w)))], axis=0)
    as_bf16 = lambda m: jnp.asarray(m.astype(np.float32)).astype(BF16)
    return as_bf16(a1), as_bf16(a2), as_bf16(d2)


def _fourier_kernel(f_ref, a1_ref, a2_ref, d2_ref, o_ref, yre_ref, yim_ref, p_ref, q_ref, *, cols, tm):
    n_tok, df = f_ref.shape[1], f_ref.shape[2]
    rows = n_tok // cols
    for t0 in range(0, n_tok, COL_GROUP):
        y = jnp.dot(a1_ref[...], f_ref[0, t0:t0 + COL_GROUP, :], preferred_element_type=F32)
        yre_ref[t0:t0 + COL_GROUP, :] = y[:COL_GROUP]
        yim_ref[t0:t0 + COL_GROUP, :] = y[COL_GROUP:]
    n_grp = rows * ROW_GROUP
    for c0 in range(0, cols, ROW_GROUP):
        slabs = [slice(r * cols + c0, r * cols + c0 + ROW_GROUP) for r in range(rows)]
        y = jnp.concatenate([yre_ref[sl, :] for sl in slabs] + [yim_ref[sl, :] for sl in slabs], axis=0)
        z = jnp.dot(a2_ref[...], y.astype(BF16), preferred_element_type=F32)
        for r, sl in enumerate(slabs):
            p_ref[sl, :] = z[r * ROW_GROUP:(r + 1) * ROW_GROUP]
            q_ref[sl, :] = z[n_grp + r * ROW_GROUP:n_grp + (r + 1) * ROW_GROUP]
    for m in range(0, n_tok, tm):
        o = jnp.dot(p_ref[m:m + tm, :].astype(BF16), d2_ref[:df, :], preferred_element_type=F32)
        o = o + jnp.dot(q_ref[m:m + tm, :].astype(BF16), d2_ref[df:, :], preferred_element_type=F32)
        o_ref[0, m:m + tm, :] = o.astype(o_ref.dtype)


def _fourier(f, a1, a2, d2, cols, tm=512):
    bsz, n_tok, df = f.shape
    return pl.pallas_call(
        functools.partial(_fourier_kernel, cols=cols, tm=tm),
        grid=(bsz,),
        in_specs=[pl.BlockSpec((1, n_tok, df), lambda b: (b, 0, 0)),
                  _resident(a1.shape), _resident(a2.shape), _resident(d2.shape)],
        out_specs=pl.BlockSpec((1, n_tok, df), lambda b: (b, 0, 0)),
        out_shape=jax.ShapeDtypeStruct(f.shape, BF16),
        scratch_shapes=[pltpu.VMEM((n_tok, df), F32) for _ in range(4)],
        compiler_params=_params(("arbitrary",)),
        name="fourier",
    )(f, a1, a2, d2)


def _mid_kernel(x_ref, sh_ref, sc_ref, g_ref, w_ref, cw_ref, cb_ref, wg_ref, br_ref, bi_ref, lam_ref, h0_ref,
                f_ref, l_ref, gt_ref, hf_ref, ext_ref, carry_ref,
                *, tm, nt, n_tiles, half, chunk, splits):
    s = pl.program_id(0)
    ip = jnp.minimum(s, n_tiles - 1) % nt
    ir = jnp.maximum(s - 1, 0) % nt
    n_slab = ext_ref.shape[0]
    (f0, fw), (l0, lw), (g0, gw), (ga0, gaw), (gb0, gbw) = splits

    @pl.when(s == 0)
    def _():
        ext_ref[...] = jnp.zeros(ext_ref.shape, F32)
        carry_ref[...] = jnp.zeros(carry_ref.shape, F32)

    h = (_rms(x_ref[0], g_ref[...]) * (1.0 + sc_ref[0]) + sh_ref[0]).astype(BF16)

    def project(o_ref, dst, start, c, act):
        r = jnp.dot(h, w_ref[:, start + c:start + c + chunk], preferred_element_type=F32)
        if o_ref is l_ref:
            for k in range(chunk // LANES):
                l_ref[0, c // LANES + k] = r[:, k * LANES:(k + 1) * LANES]
        else:
            o_ref[0, :, dst + c:dst + c + chunk] = (r if act is None else act(r)).astype(o_ref.dtype)

    c1 = _lru_c1(lam_ref[...])
    rows = lax.broadcasted_iota(jnp.int32, (SUBLANES, LANES), 0)
    carries = []
    for j in range(n_slab):
        sl = slice(j * LANES, (j + 1) * LANES)
        h0 = jnp.broadcast_to(h0_ref[0, :, sl], (SUBLANES, LANES))
        carries.append(jnp.where(ir == 0, h0, carry_ref[j]))

    def close_window():
        for j in range(n_slab):
            ext_ref[j, SUBLANES + tm:2 * SUBLANES + tm, :] = jnp.where(ir == nt - 1, 0.0, l_ref[0, j, 0:SUBLANES, :])

    xcs, gates = {}, {}

    def conv_piece(row0, j):
        sl = slice(j * LANES, (j + 1) * LANES)
        xcs[row0, j] = _lru_conv(ext_ref, j, row0, half, cw_ref[:, sl], cb_ref[:, sl])

    def gate_piece(row0, j):
        sl = slice(j * LANES, (j + 1) * LANES)
        gates[row0, j] = _lru_gate_math(xcs.pop((row0, j)), wg_ref[j], br_ref[:, sl], bi_ref[:, sl], c1[:, sl])

    def scan_piece(row0, j):
        a, b = gates.pop((row0, j))
        for base in range(0, half, SCAN_ROWS):
            carries[j] = _scan_group(a, b, base, hf_ref.at[0], j, row0 + base, carries[j], rows, False)

    scan_pieces = []
    for row0 in range(0, tm, half):
        if row0 + half == tm:
            scan_pieces.append(close_window)
        scan_pieces.append(functools.partial(conv_piece, row0, 0))
        for j in range(n_slab):
            if j + 1 < n_slab:
                scan_pieces.append(functools.partial(conv_piece, row0, j + 1))
            scan_pieces.append(functools.partial(gate_piece, row0, j))
            scan_pieces.append(functools.partial(scan_piece, row0, j))

    proj_pieces = [functools.partial(project, o_ref, dst, start, c, act)
                   for o_ref, dst, start, width, act in (
                       (l_ref, 0, l0, lw, None), (f_ref, 0, f0, fw, None), (gt_ref, 0, g0, gw, _gelu_tanh),
                       (gt_ref, gw, ga0, gaw, _sigmoid), (gt_ref, gw + gaw, gb0, gbw, _sigmoid))
                   for c in range(0, width, chunk)]
    assert half < tm and lw // chunk <= len(proj_pieces) // 2
    for piece in _interleave(scan_pieces, proj_pieces):
        piece()

    for j in range(n_slab):
        carry_ref[j] = carries[j]
        ext_ref[j, 0:SUBLANES, :] = jnp.where(ip == 0, 0.0, ext_ref[j, tm:tm + SUBLANES, :])
        ext_ref[j, SUBLANES:SUBLANES + tm, :] = l_ref[0, j]


def _mid(x, mod, k_shift, g, w, splits, conv_w, conv_b, wg, br, bi, lam, h0, tm=512, half=128, chunk=MXU_DIM):
    bsz, seq, d = x.shape
    ch = splits[1][1]
    nt = seq // tm
    n_tiles = bsz * nt
    n_slab = ch // LANES

    def proj_tile(s):
        p = jnp.minimum(s, n_tiles - 1)
        return p // nt, p % nt

    def scan_tile(s):
        r = jnp.maximum(s - 1, 0)
        return r // nt, r % nt

    tok = lambda wd: pl.BlockSpec((1, tm, wd), lambda s: proj_tile(s) + (0,))
    slabs = lambda tile: pl.BlockSpec((1, n_slab, tm, LANES), lambda s: (tile(s)[0], 0, tile(s)[1], 0))
    mod_spec = lambda k: pl.BlockSpec((1, 1, d), lambda s: (proj_tile(s)[0], 0, k))
    w_f, w_gates = splits[0][1], sum(wd for _, wd in splits[2:])
    slab_shape = jax.ShapeDtypeStruct((bsz, n_slab, seq, LANES), F32)
    return pl.pallas_call(
        functools.partial(_mid_kernel, tm=tm, nt=nt, n_tiles=n_tiles, half=half, chunk=chunk, splits=splits),
        grid=(n_tiles + 1,),
        in_specs=[tok(d), mod_spec(k_shift), mod_spec(k_shift + 1), _resident((1, d)), _resident(w.shape),
                  _resident(conv_w.shape), _resident((1, ch)), _resident(wg.shape),
                  _resident((1, ch)), _resident((1, ch)), _resident((1, ch)),
                  pl.BlockSpec((1, 1, ch), lambda s: (scan_tile(s)[0], 0, 0))],
        out_specs=[tok(w_f), slabs(proj_tile), tok(w_gates), slabs(scan_tile)],
        out_shape=[jax.ShapeDtypeStruct((bsz, seq, w_f), BF16), slab_shape,
                   jax.ShapeDtypeStruct((bsz, seq, w_gates), BF16), slab_shape],
        scratch_shapes=[pltpu.VMEM((n_slab, tm + 2 * SUBLANES, LANES), F32),
                        pltpu.VMEM((n_slab, SUBLANES, LANES), F32)],
        compiler_params=_params(("arbitrary",)),
        name="mid",
    )(x, mod, mod, g, w, conv_w, conv_b, wg, br, bi, lam, h0)


def _tail_kernel(l_ref, lp_ref, ln_ref, cw_ref, cb_ref, wg_ref, br_ref, bi_ref, lam_ref, h0_ref,
                 x_ref, hf_ref, gt_ref, fo_ref,
                 gate2_ref, sh3_ref, sc3_ref, gate3_ref, gn_ref, gpre_ref, gpost_ref,
                 wfa_ref, wfb_ref, wo_ref, wi2_ref, wo2_ref,
                 o_ref, ext_ref, hb_ref, carry_ref, act_ref,
                 *, tm, nt, n_tiles, half, chunk, merge_split):
    s = pl.program_id(0)
    q = jnp.minimum(s, n_tiles - 1)
    cq = nt - 1 - q % nt
    n_slab = hb_ref.shape[0]
    ch, d = n_slab * LANES, x_ref.shape[2]

    @pl.when(s == 0)
    def _():
        hb_ref[...] = jnp.zeros(hb_ref.shape, F32)
        carry_ref[...] = jnp.zeros(carry_ref.shape, F32)

    blocks = [slice(r0, r0 + tm // merge_split) for r0 in range(0, tm, tm // merge_split)]
    yas = [jnp.dot(fo_ref[0, rs, :], wfa_ref[...], preferred_element_type=F32) for rs in blocks]
    ybs = []
    for rs in blocks:
        hsum = jnp.concatenate([hf_ref[0, j, rs, :] + hb_ref[j, rs, :] for j in range(n_slab)], axis=1)
        u = (hsum * gt_ref[0, rs, 0:ch].astype(F32)).astype(BF16)
        ybs.append(jnp.dot(u, wfb_ref[...], preferred_element_type=F32))
    mxs = []
    for rs, ya, yb in zip(blocks, yas, ybs):
        m = gt_ref[0, rs, ch:ch + d].astype(F32) * ya + gt_ref[0, rs, ch + d:ch + 2 * d].astype(F32) * yb
        mxs.append(jnp.dot(m.astype(BF16), wo_ref[...], preferred_element_type=F32))
    x2s = [x_ref[0, rs, :] + gate2_ref[0] * _rms(mx, gn_ref[...]) for rs, mx in zip(blocks, mxs)]
    hs = [_ffn_pre(x2, sh3_ref[0], sc3_ref[0], gpre_ref[...]) for x2 in x2s]
    x2 = jnp.concatenate(x2s, axis=0)
    h2 = jnp.concatenate(hs, axis=0)

    c1 = _lru_c1(lam_ref[...])
    rows = lax.broadcasted_iota(jnp.int32, (SUBLANES, LANES), 0)
    carries = []
    for j in range(n_slab):
        sl = slice(j * LANES, (j + 1) * LANES)
        _lru_fill_ext(ext_ref.at[0], j, lp_ref[0, j], l_ref[0, j, 0:EDGE, :], l_ref[0, j, EDGE:EDGE + SUBLANES, :],
                      cq == 0, False)
        _lru_fill_ext(ext_ref.at[1], j, l_ref[0, j, tm - EDGE - SUBLANES:tm - EDGE, :], l_ref[0, j, tm - EDGE:tm, :],
                      ln_ref[0, j], False, cq == nt - 1)
        h0 = jnp.broadcast_to(h0_ref[0, :, sl], (SUBLANES, LANES))
        carries.append(jnp.where(cq == nt - 1, h0, carry_ref[j]))

    span = SEG_LEN * (SUBLANES - 1)

    def tile_rows(j, r):
        if r < 0:
            return ext_ref[0, j, pl.ds(SUBLANES + r, SUBLANES, stride=SEG_LEN), :]
        if r + span >= tm:
            return ext_ref[1, j, pl.ds(SUBLANES + r - (tm - EDGE), SUBLANES, stride=SEG_LEN), :]
        return l_ref[0, j, pl.ds(r, SUBLANES, stride=SEG_LEN), :]

    xcs, gates = {}, {}

    def conv_piece(row0, j):
        sl = slice(j * LANES, (j + 1) * LANES)
        xcs[row0, j] = _lru_conv(None, j, row0, half, cw_ref[:, sl], cb_ref[:, sl],
                                 functools.partial(tile_rows, j))

    def gate_piece(row0, j):
        sl = slice(j * LANES, (j + 1) * LANES)
        gates[row0, j] = _lru_gate_math(xcs.pop((row0, j)), wg_ref[j], br_ref[:, sl], bi_ref[:, sl], c1[:, sl])

    def scan_piece(row0, j):
        a, b = gates.pop((row0, j))
        for base in range(half - SCAN_ROWS, -1, -SCAN_ROWS):
            carries[j] = _scan_group(a, b, base, hb_ref, j, row0 + base, carries[j], rows, True)

    pieces = []
    for row0 in range(tm - half, -1, -half):
        pieces.append(functools.partial(conv_piece, row0, 0))
        for j in range(n_slab):
            if j + 1 < n_slab:
                pieces.append(functools.partial(conv_piece, row0, j + 1))
            pieces.append(functools.partial(gate_piece, row0, j))
            pieces.append(functools.partial(scan_piece, row0, j))

    def between(i, n):
        for piece in pieces[i * len(pieces) // n:(i + 1) * len(pieces) // n]:
            piece()

    _ffn_hidden(h2, wi2_ref, act_ref, chunk, between)
    o_ref[0] = _ffn_post(x2, gate3_ref[0], gpost_ref[...], act_ref, wo2_ref)
    for j in range(n_slab):
        carry_ref[j] = carries[j]


def _tail(l, conv_w, conv_b, wg, br, bi, lam, h0, x, hf, gates, fo, mod, gn, gpre, gpost,
          wfa, wfb, wo, wi2, wo2, tm=512, half=128, chunk=MXU_DIM, merge_split=2):
    bsz, seq, d = x.shape
    ch = l.shape[1] * LANES
    nt = seq // tm
    n_tiles = bsz * nt
    n_slab = ch // LANES
    dff = wo2.shape[0]

    def scan_tile(s):
        q = jnp.minimum(s, n_tiles - 1)
        return q // nt, nt - 1 - q % nt

    def merge_tile(s):
        r = jnp.clip(s - 1, 0, n_tiles - 1)
        return r // nt, nt - 1 - r % nt

    tok = lambda w: pl.BlockSpec((1, tm, w), lambda s: merge_tile(s) + (0,))
    mod_spec = lambda k: pl.BlockSpec((1, 1, d), lambda s: (merge_tile(s)[0], 0, k))
    return pl.pallas_call(
        functools.partial(_tail_kernel, tm=tm, nt=nt, n_tiles=n_tiles, half=half, chunk=chunk,
                          merge_split=merge_split),
        grid=(n_tiles + 1,),
        in_specs=_halo_specs(tm, seq, ch, scan_tile) + [
            _resident(conv_w.shape), _resident((1, ch)), _resident(wg.shape),
            _resident((1, ch)), _resident((1, ch)), _resident((1, ch)),
            pl.BlockSpec((1, 1, ch), lambda s: (scan_tile(s)[0], 0, 0)),
            tok(d),
            pl.BlockSpec((1, n_slab, tm, LANES), lambda s: (merge_tile(s)[0], 0, merge_tile(s)[1], 0)),
            tok(gates.shape[2]), tok(fo.shape[2]),
            mod_spec(5), mod_spec(6), mod_spec(7), mod_spec(8),
            _resident((1, d)), _resident((1, d)), _resident((1, d)),
            _resident(wfa.shape), _resident(wfb.shape), _resident(wo.shape),
            _resident(wi2.shape), _resident(wo2.shape)],
        out_specs=tok(d),
        out_shape=jax.ShapeDtypeStruct(x.shape, F32),
        scratch_shapes=[pltpu.VMEM((2, n_slab, EDGE + 2 * SUBLANES, LANES), F32),
                        pltpu.VMEM((n_slab, tm, LANES), F32),
                        pltpu.VMEM((n_slab, SUBLANES, LANES), F32),
                        pltpu.VMEM((tm, dff), BF16)],
        compiler_params=_params(("arbitrary",), TAIL_VMEM_LIMIT_BYTES),
        name="tail",
    )(l, l, l, conv_w, conv_b, wg, br, bi, lam, h0,
      x, hf, gates, fo, mod, mod, mod, mod, gn, gpre, gpost, wfa, wfb, wo, wi2, wo2)


def kernel(x, c, ctx, c_ctx, w_ada, b_ada, norm_g, w_ffn1_in, w_ffn1_out, w_ffn2_in, w_ffn2_out,
           w_in, conv_w, conv_b, w_r, b_r, w_i, b_i, lam, w_fa, w_fb, w_out):
    depth = w_ada.shape[0]
    assert depth == 1, "single-layer problem: the context stream is only needed up to the mixer scans"
    bsz, seq, d = x.shape
    d_lru = w_fb.shape[1]
    d_f = w_fa.shape[1]
    lyr = 0

    pad = (-(bsz + 1)) % SUBLANES
    cc = jnp.concatenate([c, c_ctx[None, :], jnp.zeros((pad, d), F32)], axis=0)
    mod = _ada(cc, w_ada[lyr], b_ada[lyr][None, :])
    mod = mod.reshape(mod.shape[0], 1, N_MOD * d)
    lat_row = lambda b: b
    ctx_row = lambda b: bsz
    g = norm_g[lyr][:, None, :]

    wi1, wo1 = w_ffn1_in[lyr].astype(BF16), w_ffn1_out[lyr].astype(BF16)

    x1, wi2, wo2, w_in_b, wfa_b, wfb_b, wout_b = _ffn(
        x, mod, lat_row, 0, g[0], g[1], wi1, wo1,
        casts=(w_ffn2_in[lyr], w_ffn2_out[lyr], w_in[lyr], w_fa[lyr], w_fb[lyr], w_out[lyr]), tm=1024)
    ctx_flat = ctx.reshape(1, bsz * ctx.shape[1], d)
    c1 = _ffn(ctx_flat, mod, ctx_row, 0, g[0], g[1], wi1, wo1)

    wgate = (0.5 * jnp.concatenate([w_r[lyr], w_i[lyr]], axis=-1)).astype(BF16)
    br, bi, lm = 0.5 * b_r[lyr][:, None, :], 0.5 * b_i[lyr][:, None, :], lam[lyr][:, None, :]
    cw, cb = conv_w[lyr], conv_b[lyr][None, :]
    h0f, h0b = _ctx(c1.reshape(ctx.shape), mod, bsz, 3, g[2], w_in_b, d_f, cw, cb, wgate, br, bi, lm)

    splits = ((0, d_f), (d_f, d_lru), (d_f + d_lru, d_lru), (d_f + 2 * d_lru, d), (d_f + 2 * d_lru + d, d))
    fx, lx, gates, hf = _mid(x1, mod, 3, g[2], w_in_b, splits, cw, cb, wgate[0], br[0], bi[0], lm[0], h0f)

    a1, a2, d2 = _dft_tables(seq // GRID_W, GRID_W, d_f // FOURIER_GROUPS, FOURIER_GROUPS)
    fo = _fourier(fx, a1, a2, d2, GRID_W)

    return _tail(lx, cw, cb, wgate[1], br[1], bi[1], lm[1], h0b, x1, hf, gates, fo, mod,
                 g[3], g[4], g[5], wfa_b, wfb_b, wout_b, wi2, wo2)
```

```python
import functools
import math

import numpy as np
import jax
import jax.numpy as jnp
from jax import lax
from jax.experimental import pallas as pl
from jax.experimental.pallas import tpu as pltpu

LANES = 128
SUBLANES = 8
MXU_DIM = 256
VMEM_LIMIT_BYTES = 56 * 1024 * 1024
TAIL_VMEM_LIMIT_BYTES = 62 * 1024 * 1024

GRID_W = 64
FOURIER_GROUPS = 4
CONV_W = 4
CONV_LEFT = (CONV_W - 1) // 2
GATE_C = 8.0
N_MOD = 9
EPS = 1e-6

BF16 = jnp.bfloat16
F32 = jnp.float32

SEG_LEN = 4
SCAN_ROWS = SEG_LEN * SUBLANES


def _sigmoid(x):
    return 0.5 * jnp.tanh(0.5 * x) + 0.5


def _silu(x):
    return x * _sigmoid(x)


def _gelu_tanh(x):
    c = math.sqrt(2.0 / math.pi)
    return 0.5 * x * (1.0 + jnp.tanh(c * (x + 0.044715 * (x * x * x))))


def _rms(x, g):
    return x * lax.rsqrt(jnp.mean(x * x, axis=-1, keepdims=True) + EPS) * g


def _resident(shape):
    nd = len(shape)
    return pl.BlockSpec(shape, lambda *_: (0,) * nd, pipeline_mode=pl.Buffered(1))


def _params(semantics, vmem=VMEM_LIMIT_BYTES):
    return pltpu.CompilerParams(dimension_semantics=semantics, vmem_limit_bytes=vmem)


def _interleave(major, minor):
    merged, done = [], 0
    for i, piece in enumerate(major):
        while done < len(minor) and done * len(major) <= i * len(minor):
            merged.append(minor[done])
            done += 1
        merged.append(piece)
    return merged + list(minor[done:])


def _ada_kernel(c_ref, w_ref, b_ref, o_ref):
    s = _silu(c_ref[...]).astype(BF16)
    o_ref[...] = jnp.dot(s, w_ref[...].astype(BF16), preferred_element_type=F32) + b_ref[...]


def _ada(cc, w, b, tn=1536):
    m, d = cc.shape
    n = w.shape[1]
    return pl.pallas_call(
        _ada_kernel,
        grid=(n // tn,),
        in_specs=[pl.BlockSpec((m, d), lambda j: (0, 0)),
                  pl.BlockSpec((d, tn), lambda j: (0, j)),
                  pl.BlockSpec((1, tn), lambda j: (0, j))],
        out_specs=pl.BlockSpec((m, tn), lambda j: (0, j)),
        out_shape=jax.ShapeDtypeStruct((m, n), F32),
        compiler_params=_params(("arbitrary",)),
        name="adaln",
    )(cc, w, b)


def _ffn_pre(x, sh, sc, gpre):
    return (_rms(x, gpre) * (1.0 + sc) + sh).astype(BF16)


def _ffn_hidden(h, wi_ref, act_ref, chunk, between=None):
    dff = act_ref.shape[1]
    for i, c in enumerate(range(0, dff, chunk)):
        gate = jnp.dot(h[...], wi_ref[:, c:c + chunk], preferred_element_type=F32)
        up = jnp.dot(h[...], wi_ref[:, dff + c:dff + c + chunk], preferred_element_type=F32)
        act_ref[:, c:c + chunk] = (_silu(gate) * up).astype(BF16)
        if between is not None:
            between(i, dff // chunk)


def _ffn_post(x, ga, gpost, act_ref, wo_ref):
    y = jnp.dot(act_ref[...], wo_ref[...], preferred_element_type=F32)
    return x + (0.5 * ga) * _rms(y, gpost)


def _ffn_kernel(x_ref, sh_ref, sc_ref, ga_ref, gpre_ref, gpost_ref, wi_ref, wo_ref, *rest, chunk, n_cast):
    cast_in, o_ref, cast_out, act_ref = rest[:n_cast], rest[n_cast], rest[n_cast + 1:-1], rest[-1]
    x = x_ref[0]
    _ffn_hidden(_ffn_pre(x, sh_ref[0], sc_ref[0], gpre_ref[...]), wi_ref, act_ref, chunk)
    o_ref[0] = _ffn_post(x, ga_ref[0], gpost_ref[...], act_ref, wo_ref)
    for src, dst in zip(cast_in, cast_out):
        dst[...] = src[...].astype(dst.dtype)


def _ffn(x, mod, mod_row, k_shift, g_pre, g_post, wi, wo, casts=(), tm=512, chunk=MXU_DIM):
    bsz, seq, d = x.shape
    dff = wo.shape[0]
    nt = seq // tm
    n_steps = bsz * nt
    mod_spec = lambda k: pl.BlockSpec((1, 1, d), lambda b, i: (mod_row(b), 0, k))

    def cast_spec(w):
        rows = w.shape[0]
        n_blocks = max(n for n in range(1, n_steps + 1)
                       if n_steps % n == 0 and rows % n == 0 and (rows // n) % (2 * SUBLANES) == 0)
        rep = n_steps // n_blocks
        return pl.BlockSpec((rows // n_blocks, w.shape[1]), lambda b, i: ((b * nt + i) // rep, 0))

    cast_specs = lambda: [cast_spec(w) for w in casts]
    out = pl.pallas_call(
        functools.partial(_ffn_kernel, chunk=chunk, n_cast=len(casts)),
        grid=(bsz, nt),
        in_specs=[pl.BlockSpec((1, tm, d), lambda b, i: (b, i, 0)),
                  mod_spec(k_shift), mod_spec(k_shift + 1), mod_spec(k_shift + 2),
                  _resident((1, d)), _resident((1, d)),
                  _resident(wi.shape), _resident(wo.shape)] + cast_specs(),
        out_specs=[pl.BlockSpec((1, tm, d), lambda b, i: (b, i, 0))] + cast_specs(),
        out_shape=[jax.ShapeDtypeStruct(x.shape, F32)] + [jax.ShapeDtypeStruct(w.shape, BF16) for w in casts],
        scratch_shapes=[pltpu.VMEM((tm, dff), BF16)],
        compiler_params=_params(("arbitrary", "arbitrary")),
        name="ffn",
    )(x, mod, mod, mod, g_pre, g_post, wi, wo, *casts)
    return out if casts else out[0]


def _lru_c1(lam):
    e = jnp.exp(-jnp.abs(lam))
    u = 1.0 + e
    log1p_e = jnp.where(u == 1.0, e, jnp.log(u) * (e / (u - 1.0)))
    return (-0.5 * GATE_C * math.log2(math.e)) * (jnp.maximum(-lam, 0.0) + log1p_e)


def _lru_fill_ext(ext_ref, j, prev, main, nxt, at_start, at_end):
    tt = main.shape[0]
    ext_ref[j, 0:SUBLANES, :] = jnp.where(at_start, 0.0, prev)
    ext_ref[j, SUBLANES:SUBLANES + tt, :] = main
    ext_ref[j, SUBLANES + tt:2 * SUBLANES + tt, :] = jnp.where(at_end, 0.0, nxt)


def _lru_conv(ext_ref, j, row0, nrows, cw, cb):
    xc = cb
    for k in range(CONV_W):
        off = SUBLANES - CONV_LEFT + k + row0
        tap = jnp.concatenate(
            [ext_ref[j, pl.ds(off + m + kk, SUBLANES, stride=SEG_LEN), :]
             for m in range(0, nrows, SCAN_ROWS) for kk in range(SEG_LEN)], axis=0)
        xc = xc + tap * cw[k:k + 1, :]
    return xc


def _lru_gate_math(xc, wg, br, bi, c1):
    z = jnp.dot(xc.astype(BF16), wg, preferred_element_type=F32)
    t_r = jnp.tanh(z[:, :LANES] + br)
    t_i = jnp.tanh(z[:, LANES:] + bi)
    a = jnp.exp2(c1 * t_r + c1)
    y = 1.0 - a * a
    root = jnp.where(y > 0.0, y * lax.rsqrt(y), 0.0)
    hx = 0.5 * xc
    return a, root * (t_i * hx + hx)


def _shift_rows(x, k, fill, rows):
    return jnp.where(rows >= k, pltpu.roll(x, k, axis=0), fill)


def _shift_rows_up(x, k, fill, rows):
    return jnp.where(rows < SUBLANES - k, pltpu.roll(x, SUBLANES - k, axis=0), fill)


def _scan_group(a_all, b_all, base, o_ref, j, out_base, carry, rows, reverse):
    ks = range(SEG_LEN - 1, -1, -1) if reverse else range(SEG_LEN)
    shift = _shift_rows_up if reverse else _shift_rows
    a = {k: a_all[base + k * SUBLANES:base + (k + 1) * SUBLANES, :] for k in ks}
    b = {k: b_all[base + k * SUBLANES:base + (k + 1) * SUBLANES, :] for k in ks}
    h, p = {}, {}
    prev = None
    for k in ks:
        if prev is None:
            h[k], p[k] = b[k], a[k]
        else:
            h[k] = a[k] * h[prev] + b[k]
            p[k] = a[k] * p[prev]
        prev = k
    hc, pc = h[prev], p[prev]
    for dd in (1, 2, 4):
        hc = pc * shift(hc, dd, 0.0, rows) + hc
        pc = pc * shift(pc, dd, 1.0, rows)
    g = hc + pc * carry
    if o_ref is not None:
        cin = shift(g, 1, carry, rows)
        for k in ks:
            o_ref[j, pl.ds(out_base + k, SUBLANES, stride=SEG_LEN), :] = h[k] + p[k] * cin
    last = 0 if reverse else SUBLANES - 1
    return jnp.broadcast_to(g[last:last + 1, :], (SUBLANES, LANES))


def _ctx_kernel(c_ref, sh_ref, sc_ref, g_ref, w_ref, cw_ref, cb_ref, wg_ref, br_ref, bi_ref, lam_ref,
                hf_ref, hb_ref, ext_ref, *, l0):
    tt = c_ref.shape[1]
    n_slab = ext_ref.shape[0]
    h = (_rms(c_ref[0], g_ref[...]) * (1.0 + sc_ref[0]) + sh_ref[0]).astype(BF16)
    halo = jnp.zeros((SUBLANES, LANES), F32)
    rows = lax.broadcasted_iota(jnp.int32, (SUBLANES, LANES), 0)
    for j0 in range(0, n_slab, MXU_DIM // LANES):
        cols = slice(l0 + j0 * LANES, l0 + j0 * LANES + MXU_DIM)
        l_pair = jnp.dot(h, w_ref[:, cols], preferred_element_type=F32)
        for j in range(j0, j0 + MXU_DIM // LANES):
            sl = slice(j * LANES, (j + 1) * LANES)
            _lru_fill_ext(ext_ref, j, halo, l_pair[:, (j - j0) * LANES:(j - j0 + 1) * LANES], halo, False, False)
            xc = _lru_conv(ext_ref, j, 0, tt, cw_ref[:, sl], cb_ref[:, sl])
            for d, (o_ref, reverse) in enumerate(((hf_ref, False), (hb_ref, True))):
                a, b = _lru_gate_math(xc, wg_ref[d, j], br_ref[d, :, sl], bi_ref[d, :, sl],
                                      _lru_c1(lam_ref[d, :, sl]))
                carry = jnp.zeros((SUBLANES, LANES), F32)
                for base in (range(tt - SCAN_ROWS, -1, -SCAN_ROWS) if reverse else range(0, tt, SCAN_ROWS)):
                    carry = _scan_group(a, b, base, None, j, base, carry, rows, reverse)
                o_ref[0, :, sl] = carry[0:1, :]


def _ctx(c1, mod, mod_row, k_shift, g, w, l0, conv_w, conv_b, wg, br, bi, lam):
    bsz, tt, d = c1.shape
    n_slab = wg.shape[1]
    ch = n_slab * LANES
    mod_spec = lambda k: pl.BlockSpec((1, 1, d), lambda b: (mod_row, 0, k))
    end_state = lambda: pl.BlockSpec((1, 1, ch), lambda b: (b, 0, 0))
    return pl.pallas_call(
        functools.partial(_ctx_kernel, l0=l0),
        grid=(bsz,),
        in_specs=[pl.BlockSpec((1, tt, d), lambda b: (b, 0, 0)), mod_spec(k_shift), mod_spec(k_shift + 1),
                  _resident((1, d)), _resident(w.shape), _resident(conv_w.shape), _resident((1, ch)),
                  _resident(wg.shape), _resident(br.shape), _resident(bi.shape), _resident(lam.shape)],
        out_specs=[end_state(), end_state()],
        out_shape=[jax.ShapeDtypeStruct((bsz, 1, ch), F32)] * 2,
        scratch_shapes=[pltpu.VMEM((n_slab, tt + 2 * SUBLANES, LANES), F32)],
        compiler_params=_params(("arbitrary",)),
        name="ctx_scan",
    )(c1, mod, mod, g, w, conv_w, conv_b, wg, br, bi, lam)


def _halo_specs(tt, t, ch, idx):
    hb = tt // SUBLANES
    last = t // SUBLANES - 1

    def prev_map(*g):
        b, tc = idx(*g)
        return b, jnp.maximum(tc * hb - 1, 0), 0

    def next_map(*g):
        b, tc = idx(*g)
        return b, jnp.minimum((tc + 1) * hb, last), 0

    return [pl.BlockSpec((1, tt, ch), lambda *g: idx(*g) + (0,)),
            pl.BlockSpec((1, SUBLANES, ch), prev_map),
            pl.BlockSpec((1, SUBLANES, ch), next_map)]


COL_GROUP = MXU_DIM
ROW_GROUP = SUBLANES


def _dft_tables(rows, cols, group_w, groups):
    scale = 1.0 / math.sqrt(rows * cols * group_w)
    ang = lambda n: 2.0 * np.pi * ((np.arange(n)[:, None] * np.arange(n)[None, :]) % n) / n
    eye = np.eye
    c_col, s_col = np.kron(eye(COL_GROUP // cols), np.cos(ang(cols))), np.kron(eye(COL_GROUP // cols), np.sin(ang(cols)))
    a1 = np.concatenate([c_col, -s_col], axis=0)
    c_row, s_row = np.kron(np.cos(ang(rows)), eye(ROW_GROUP)) * scale, np.kron(np.sin(ang(rows)), eye(ROW_GROUP)) * scale
    a2 = np.block([[c_row, s_row], [-s_row, c_row]])
    d2 = np.concatenate([np.kron(eye(groups), np.cos(ang(group_w))), np.kron(eye(groups), np.sin(ang(group_w)))], axis=0)
    as_bf16 = lambda m: jnp.asarray(m.astype(np.float32)).astype(BF16)
    return as_bf16(a1), as_bf16(a2), as_bf16(d2)


def _fourier_kernel(f_ref, a1_ref, a2_ref, d2_ref, o_ref, yre_ref, yim_ref, p_ref, q_ref, *, cols, tm):
    n_tok, df = f_ref.shape[1], f_ref.shape[2]
    rows = n_tok // cols
    for t0 in range(0, n_tok, COL_GROUP):
        y = jnp.dot(a1_ref[...], f_ref[0, t0:t0 + COL_GROUP, :], preferred_element_type=F32)
        yre_ref[t0:t0 + COL_GROUP, :] = y[:COL_GROUP]
        yim_ref[t0:t0 + COL_GROUP, :] = y[COL_GROUP:]
    n_grp = rows * ROW_GROUP
    for c0 in range(0, cols, ROW_GROUP):
        slabs = [slice(r * cols + c0, r * cols + c0 + ROW_GROUP) for r in range(rows)]
        y = jnp.concatenate([yre_ref[sl, :] for sl in slabs] + [yim_ref[sl, :] for sl in slabs], axis=0)
        z = jnp.dot(a2_ref[...], y.astype(BF16), preferred_element_type=F32)
        for r, sl in enumerate(slabs):
            p_ref[sl, :] = z[r * ROW_GROUP:(r + 1) * ROW_GROUP]
            q_ref[sl, :] = z[n_grp + r * ROW_GROUP:n_grp + (r + 1) * ROW_GROUP]
    for m in range(0, n_tok, tm):
        o = jnp.dot(p_ref[m:m + tm, :].astype(BF16), d2_ref[:df, :], preferred_element_type=F32)
        o = o + jnp.dot(q_ref[m:m + tm, :].astype(BF16), d2_ref[df:, :], preferred_element_type=F32)
        o_ref[0, m:m + tm, :] = o.astype(o_ref.dtype)


def _fourier(f, a1, a2, d2, cols, tm=512):
    bsz, n_tok, df = f.shape
    return pl.pallas_call(
        functools.partial(_fourier_kernel, cols=cols, tm=tm),
        grid=(bsz,),
        in_specs=[pl.BlockSpec((1, n_tok, df), lambda b: (b, 0, 0)),
                  _resident(a1.shape), _resident(a2.shape), _resident(d2.shape)],
        out_specs=pl.BlockSpec((1, n_tok, df), lambda b: (b, 0, 0)),
        out_shape=jax.ShapeDtypeStruct(f.shape, BF16),
        scratch_shapes=[pltpu.VMEM((n_tok, df), F32) for _ in range(4)],
        compiler_params=_params(("arbitrary",)),
        name="fourier",
    )(f, a1, a2, d2)


def _mid_kernel(x_ref, sh_ref, sc_ref, g_ref, w_ref, cw_ref, cb_ref, wg_ref, br_ref, bi_ref, lam_ref, h0_ref,
                f_ref, l_ref, gt_ref, hf_ref, ext_ref, carry_ref,
                *, tm, nt, n_tiles, half, chunk, splits):
    s = pl.program_id(0)
    ip = jnp.minimum(s, n_tiles - 1) % nt
    ir = jnp.maximum(s - 1, 0) % nt
    n_slab = ext_ref.shape[0]
    (f0, fw), (l0, lw), (g0, gw), (ga0, gaw), (gb0, gbw) = splits

    def projection_stage():
        h = (_rms(x_ref[0], g_ref[...]) * (1.0 + sc_ref[0]) + sh_ref[0]).astype(BF16)

        def project(o_ref, dst, start, c, act):
            r = jnp.dot(h, w_ref[:, start + c:start + c + chunk], preferred_element_type=F32)
            o_ref[0, :, dst + c:dst + c + chunk] = (r if act is None else act(r)).astype(o_ref.dtype)

        def roll_window():
            for j in range(n_slab):
                sl = slice(j * LANES, (j + 1) * LANES)
                ext_ref[j, 0:SUBLANES, :] = jnp.where(ip == 0, 0.0, ext_ref[j, tm:tm + SUBLANES, :])
                ext_ref[j, SUBLANES:SUBLANES + tm, :] = l_ref[0, :, sl]

        pieces = [functools.partial(project, o_ref, dst, start, c, act)
                  for o_ref, dst, start, width, act in (
                      (l_ref, 0, l0, lw, None), (f_ref, 0, f0, fw, None), (gt_ref, 0, g0, gw, _gelu_tanh),
                      (gt_ref, gw, ga0, gaw, _sigmoid), (gt_ref, gw + gaw, gb0, gbw, _sigmoid))
                  for c in range(0, width, chunk)]
        assert half < tm and lw // chunk <= len(pieces) // 2
        return pieces, roll_window

    def scan_stage(has_next):
        c1 = _lru_c1(lam_ref[...])
        rows = lax.broadcasted_iota(jnp.int32, (SUBLANES, LANES), 0)
        carries = []
        for j in range(n_slab):
            sl = slice(j * LANES, (j + 1) * LANES)
            h0 = jnp.broadcast_to(h0_ref[0, :, sl], (SUBLANES, LANES))
            carries.append(jnp.where(ir == 0, h0, carry_ref[j]))

        def close_window():
            for j in range(n_slab):
                sl = slice(j * LANES, (j + 1) * LANES)
                nxt = jnp.where(ir == nt - 1, 0.0, l_ref[0, 0:SUBLANES, sl]) if has_next else 0.0
                ext_ref[j, SUBLANES + tm:2 * SUBLANES + tm, :] = jnp.broadcast_to(nxt, (SUBLANES, LANES))

        xcs, gates = {}, {}

        def conv_piece(row0, j):
            sl = slice(j * LANES, (j + 1) * LANES)
            xcs[row0, j] = _lru_conv(ext_ref, j, row0, half, cw_ref[:, sl], cb_ref[:, sl])

        def gate_piece(row0, j):
            sl = slice(j * LANES, (j + 1) * LANES)
            gates[row0, j] = _lru_gate_math(xcs.pop((row0, j)), wg_ref[j], br_ref[:, sl], bi_ref[:, sl], c1[:, sl])

        def scan_piece(row0, j):
            a, b = gates.pop((row0, j))
            for base in range(0, half, SCAN_ROWS):
                carries[j] = _scan_group(a, b, base, hf_ref.at[0], j, row0 + base, carries[j], rows, False)

        def save():
            for j in range(n_slab):
                carry_ref[j] = carries[j]

        pieces = []
        for row0 in range(0, tm, half):
            if row0 + half == tm:
                pieces.append(close_window)
            pieces.append(functools.partial(conv_piece, row0, 0))
            for j in range(n_slab):
                if j + 1 < n_slab:
                    pieces.append(functools.partial(conv_piece, row0, j + 1))
                pieces.append(functools.partial(gate_piece, row0, j))
                pieces.append(functools.partial(scan_piece, row0, j))
        return pieces, save

    def run(scan, project):
        scan_pieces, save = scan_stage(project) if scan else ([], lambda: None)
        proj_pieces, roll_window = projection_stage() if project else ([], lambda: None)
        for piece in _interleave(scan_pieces, proj_pieces) if scan else proj_pieces:
            piece()
        save()
        roll_window()

    @pl.when(s == 0)
    def _():
        ext_ref[...] = jnp.zeros(ext_ref.shape, F32)
        carry_ref[...] = jnp.zeros(carry_ref.shape, F32)
        run(scan=False, project=True)

    @pl.when((s > 0) & (s < n_tiles))
    def _():
        run(scan=True, project=True)

    @pl.when(s == n_tiles)
    def _():
        run(scan=True, project=False)


def _mid(x, mod, k_shift, g, w, splits, conv_w, conv_b, wg, br, bi, lam, h0, tm=512, half=128, chunk=MXU_DIM):
    bsz, seq, d = x.shape
    ch = splits[1][1]
    nt = seq // tm
    n_tiles = bsz * nt
    n_slab = ch // LANES

    def proj_tile(s):
        p = jnp.minimum(s, n_tiles - 1)
        return p // nt, p % nt

    def scan_tile(s):
        r = jnp.maximum(s - 1, 0)
        return r // nt, r % nt

    tok = lambda wd: pl.BlockSpec((1, tm, wd), lambda s: proj_tile(s) + (0,))
    mod_spec = lambda k: pl.BlockSpec((1, 1, d), lambda s: (proj_tile(s)[0], 0, k))
    widths = [splits[0][1], splits[1][1], sum(wd for _, wd in splits[2:])]
    return pl.pallas_call(
        functools.partial(_mid_kernel, tm=tm, nt=nt, n_tiles=n_tiles, half=half, chunk=chunk, splits=splits),
        grid=(n_tiles + 1,),
        in_specs=[tok(d), mod_spec(k_shift), mod_spec(k_shift + 1), _resident((1, d)), _resident(w.shape),
                  _resident(conv_w.shape), _resident((1, ch)), _resident(wg.shape),
                  _resident((1, ch)), _resident((1, ch)), _resident((1, ch)),
                  pl.BlockSpec((1, 1, ch), lambda s: (scan_tile(s)[0], 0, 0))],
        out_specs=[tok(wd) for wd in widths] + [
            pl.BlockSpec((1, n_slab, tm, LANES), lambda s: (scan_tile(s)[0], 0, scan_tile(s)[1], 0))],
        out_shape=[jax.ShapeDtypeStruct((bsz, seq, wd), dt)
                   for wd, dt in zip(widths, (BF16, F32, BF16))] + [
            jax.ShapeDtypeStruct((bsz, n_slab, seq, LANES), F32)],
        scratch_shapes=[pltpu.VMEM((n_slab, tm + 2 * SUBLANES, LANES), F32),
                        pltpu.VMEM((n_slab, SUBLANES, LANES), F32)],
        compiler_params=_params(("arbitrary",)),
        name="mid",
    )(x, mod, mod, g, w, conv_w, conv_b, wg, br, bi, lam, h0)


def _tail_kernel(l_ref, lp_ref, ln_ref, cw_ref, cb_ref, wg_ref, br_ref, bi_ref, lam_ref, h0_ref,
                 x_ref, hf_ref, gt_ref, fo_ref,
                 gate2_ref, sh3_ref, sc3_ref, gate3_ref, gn_ref, gpre_ref, gpost_ref,
                 wfa_ref, wfb_ref, wo_ref, wi2_ref, wo2_ref,
                 o_ref, ext_ref, hb_ref, carry_ref, act_ref,
                 *, tm, nt, n_tiles, half, chunk, merge_split):
    s = pl.program_id(0)
    q = jnp.minimum(s, n_tiles - 1)
    cq = nt - 1 - q % nt
    n_slab = ext_ref.shape[0]
    ch, d = n_slab * LANES, x_ref.shape[2]

    def merge_stage():
        blocks = [slice(r0, r0 + tm // merge_split) for r0 in range(0, tm, tm // merge_split)]
        yas = [jnp.dot(fo_ref[0, rs, :], wfa_ref[...], preferred_element_type=F32) for rs in blocks]
        ybs = []
        for rs in blocks:
            hsum = jnp.concatenate([hf_ref[0, j, rs, :] + hb_ref[j, rs, :] for j in range(n_slab)], axis=1)
            u = (hsum * gt_ref[0, rs, 0:ch].astype(F32)).astype(BF16)
            ybs.append(jnp.dot(u, wfb_ref[...], preferred_element_type=F32))
        mxs = []
        for rs, ya, yb in zip(blocks, yas, ybs):
            m = gt_ref[0, rs, ch:ch + d].astype(F32) * ya + gt_ref[0, rs, ch + d:ch + 2 * d].astype(F32) * yb
            mxs.append(jnp.dot(m.astype(BF16), wo_ref[...], preferred_element_type=F32))
        x2s = [x_ref[0, rs, :] + gate2_ref[0] * _rms(mx, gn_ref[...]) for rs, mx in zip(blocks, mxs)]
        hs = [_ffn_pre(x2, sh3_ref[0], sc3_ref[0], gpre_ref[...]) for x2 in x2s]
        return jnp.concatenate(x2s, axis=0), jnp.concatenate(hs, axis=0)

    def scan_stage():
        c1 = _lru_c1(lam_ref[...])
        rows = lax.broadcasted_iota(jnp.int32, (SUBLANES, LANES), 0)
        carries = []
        for j in range(n_slab):
            sl = slice(j * LANES, (j + 1) * LANES)
            _lru_fill_ext(ext_ref, j, lp_ref[0, :, sl], l_ref[0, :, sl], ln_ref[0, :, sl], cq == 0, cq == nt - 1)
            h0 = jnp.broadcast_to(h0_ref[0, :, sl], (SUBLANES, LANES))
            carries.append(jnp.where(cq == nt - 1, h0, carry_ref[j]))

        xcs, gates = {}, {}

        def conv_piece(row0, j):
            sl = slice(j * LANES, (j + 1) * LANES)
            xcs[row0, j] = _lru_conv(ext_ref, j, row0, half, cw_ref[:, sl], cb_ref[:, sl])

        def gate_piece(row0, j):
            sl = slice(j * LANES, (j + 1) * LANES)
            gates[row0, j] = _lru_gate_math(xcs.pop((row0, j)), wg_ref[j], br_ref[:, sl], bi_ref[:, sl], c1[:, sl])

        def scan_piece(row0, j):
            a, b = gates.pop((row0, j))
            for base in range(half - SCAN_ROWS, -1, -SCAN_ROWS):
                carries[j] = _scan_group(a, b, base, hb_ref, j, row0 + base, carries[j], rows, True)

        def save():
            for j in range(n_slab):
                carry_ref[j] = carries[j]

        pieces = []
        for row0 in range(tm - half, -1, -half):
            pieces.append(functools.partial(conv_piece, row0, 0))
            for j in range(n_slab):
                if j + 1 < n_slab:
                    pieces.append(functools.partial(conv_piece, row0, j + 1))
                pieces.append(functools.partial(gate_piece, row0, j))
                pieces.append(functools.partial(scan_piece, row0, j))
        return pieces, save

    @pl.when(s == 0)
    def _():
        carry_ref[...] = jnp.zeros(carry_ref.shape, F32)
        pieces, save = scan_stage()
        for piece in pieces:
            piece()
        save()

    @pl.when(s > 0)
    def _():
        x2, h2 = merge_stage()
        pieces, save = scan_stage()

        def between(i, n):
            for piece in pieces[i * len(pieces) // n:(i + 1) * len(pieces) // n]:
                piece()

        _ffn_hidden(h2, wi2_ref, act_ref, chunk, between)
        o_ref[0] = _ffn_post(x2, gate3_ref[0], gpost_ref[...], act_ref, wo2_ref)
        save()


def _tail(l, conv_w, conv_b, wg, br, bi, lam, h0, x, hf, gates, fo, mod, gn, gpre, gpost,
          wfa, wfb, wo, wi2, wo2, tm=512, half=128, chunk=MXU_DIM, merge_split=2):
    bsz, seq, d = x.shape
    ch = l.shape[2]
    nt = seq // tm
    n_tiles = bsz * nt
    n_slab = ch // LANES
    dff = wo2.shape[0]

    def scan_tile(s):
        q = jnp.minimum(s, n_tiles - 1)
        return q // nt, nt - 1 - q % nt

    def merge_tile(s):
        r = jnp.clip(s - 1, 0, n_tiles - 1)
        return r // nt, nt - 1 - r % nt

    tok = lambda w: pl.BlockSpec((1, tm, w), lambda s: merge_tile(s) + (0,))
    mod_spec = lambda k: pl.BlockSpec((1, 1, d), lambda s: (merge_tile(s)[0], 0, k))
    return pl.pallas_call(
        functools.partial(_tail_kernel, tm=tm, nt=nt, n_tiles=n_tiles, half=half, chunk=chunk,
                          merge_split=merge_split),
        grid=(n_tiles + 1,),
        in_specs=_halo_specs(tm, seq, ch, scan_tile) + [
            _resident(conv_w.shape), _resident((1, ch)), _resident(wg.shape),
            _resident((1, ch)), _resident((1, ch)), _resident((1, ch)),
            pl.BlockSpec((1, 1, ch), lambda s: (scan_tile(s)[0], 0, 0)),
            tok(d),
            pl.BlockSpec((1, n_slab, tm, LANES), lambda s: (merge_tile(s)[0], 0, merge_tile(s)[1], 0)),
            tok(gates.shape[2]), tok(fo.shape[2]),
            mod_spec(5), mod_spec(6), mod_spec(7), mod_spec(8),
            _resident((1, d)), _resident((1, d)), _resident((1, d)),
            _resident(wfa.shape), _resident(wfb.shape), _resident(wo.shape),
            _resident(wi2.shape), _resident(wo2.shape)],
        out_specs=tok(d),
        out_shape=jax.ShapeDtypeStruct(x.shape, F32),
        scratch_shapes=[pltpu.VMEM((n_slab, tm + 2 * SUBLANES, LANES), F32),
                        pltpu.VMEM((n_slab, tm, LANES), F32),
                        pltpu.VMEM((n_slab, SUBLANES, LANES), F32),
                        pltpu.VMEM((tm, dff), BF16)],
        compiler_params=_params(("arbitrary",), TAIL_VMEM_LIMIT_BYTES),
        name="tail",
    )(l, l, l, conv_w, conv_b, wg, br, bi, lam, h0,
      x, hf, gates, fo, mod, mod, mod, mod, gn, gpre, gpost, wfa, wfb, wo, wi2, wo2)


def kernel(x, c, ctx, c_ctx, w_ada, b_ada, norm_g, w_ffn1_in, w_ffn1_out, w_ffn2_in, w_ffn2_out,
           w_in, conv_w, conv_b, w_r, b_r, w_i, b_i, lam, w_fa, w_fb, w_out):
    depth = w_ada.shape[0]
    assert depth == 1, "single-layer problem: the context stream is only needed up to the mixer scans"
    bsz, seq, d = x.shape
    d_lru = w_fb.shape[1]
    d_f = w_fa.shape[1]
    lyr = 0

    pad = (-(bsz + 1)) % SUBLANES
    cc = jnp.concatenate([c, c_ctx[None, :], jnp.zeros((pad, d), F32)], axis=0)
    mod = _ada(cc, w_ada[lyr], b_ada[lyr][None, :])
    mod = mod.reshape(mod.shape[0], 1, N_MOD * d)
    lat_row = lambda b: b
    ctx_row = lambda b: bsz
    g = norm_g[lyr][:, None, :]

    wi1, wo1 = w_ffn1_in[lyr].astype(BF16), w_ffn1_out[lyr].astype(BF16)

    x1, wi2, wo2, w_in_b, wfa_b, wfb_b, wout_b = _ffn(
        x, mod, lat_row, 0, g[0], g[1], wi1, wo1,
        casts=(w_ffn2_in[lyr], w_ffn2_out[lyr], w_in[lyr], w_fa[lyr], w_fb[lyr], w_out[lyr]), tm=1024)
    ctx_flat = ctx.reshape(1, bsz * ctx.shape[1], d)
    c1 = _ffn(ctx_flat, mod, ctx_row, 0, g[0], g[1], wi1, wo1)

    wgate = (0.5 * jnp.concatenate([w_r[lyr], w_i[lyr]], axis=-1)).astype(BF16)
    br, bi, lm = 0.5 * b_r[lyr][:, None, :], 0.5 * b_i[lyr][:, None, :], lam[lyr][:, None, :]
    cw, cb = conv_w[lyr], conv_b[lyr][None, :]
    h0f, h0b = _ctx(c1.reshape(ctx.shape), mod, bsz, 3, g[2], w_in_b, d_f, cw, cb, wgate, br, bi, lm)

    splits = ((0, d_f), (d_f, d_lru), (d_f + d_lru, d_lru), (d_f + 2 * d_lru, d), (d_f + 2 * d_lru + d, d))
    fx, lx, gates, hf = _mid(x1, mod, 3, g[2], w_in_b, splits, cw, cb, wgate[0], br[0], bi[0], lm[0], h0f)

    a1, a2, d2 = _dft_tables(seq // GRID_W, GRID_W, d_f // FOURIER_GROUPS, FOURIER_GROUPS)
    fo = _fourier(fx, a1, a2, d2, GRID_W)

    return _tail(lx, cw, cb, wgate[1], br[1], bi[1], lm[1], h0b, x1, hf, gates, fo, mod,
                 g[3], g[4], g[5], wfa_b, wfb_b, wout_b, wi2, wo2)
```
